```python
import jax, jax.numpy as jnp
from jax import lax
import numpy as np

D_MODEL = 2048
BATCH = 1
SEQ = 16384
DEPTH = 2

D_HGRN = D_MODEL // 2
HGRN_HEAD_DIM = 128
HGRN_HEADS = D_HGRN // HGRN_HEAD_DIM
D_CONV = D_MODEL - D_HGRN
D_MIX = D_HGRN + D_CONV
CONV_WIDTH = 3
CHUNK = 64
D_FF = -(-8 * D_MODEL // (3 * 256)) * 256
EPS = 1e-6
SPLIT_SIZES = (D_HGRN, D_HGRN, D_HGRN, D_HGRN, D_HGRN, D_CONV, D_CONV, D_CONV)
D_IN_PROJ = sum(SPLIT_SIZES)
SPLIT_OFFSETS = [int(o) for o in np.cumsum(SPLIT_SIZES)[:-1]]

kernel_name = "bidir_hgrn2_shortconv_hybrid"


def _rmsnorm(x, w):
    x32 = x.astype(jnp.float32)
    y = x32 * lax.rsqrt(jnp.mean(x32 * x32, axis=-1, keepdims=True) + EPS)
    return (y * w.astype(jnp.float32)).astype(x.dtype)


def _lower_bounds(lb_param):
    c = jnp.cumsum(jax.nn.softmax(lb_param.astype(jnp.float32), axis=0), axis=0)
    return c - c[0:1]


def _chunk_recurrence(q, k, v, log_f):
    bsz, s, h, dk = q.shape
    dv = v.shape[-1]
    n = s // CHUNK

    def to_chunks(t):
        return t.reshape(bsz, n, CHUNK, h, t.shape[-1]).transpose(1, 0, 3, 2, 4)

    mask = jnp.tril(jnp.ones((CHUNK, CHUNK), dtype=bool))[:, :, None]

    def step(state, inp):
        qi, ki, vi, gi = inp
        b = jnp.cumsum(gi, axis=-2)
        o_inter = jnp.einsum('bhck,bhkv->bhcv', qi * jnp.exp(b), state)
        diff = b[..., :, None, :] - b[..., None, :, :]
        decay = jnp.exp(jnp.where(mask, diff, -jnp.inf))
        scores = jnp.einsum('bhtk,bhsk,bhtsk->bhts', qi, ki, decay)
        o_intra = jnp.einsum('bhts,bhsv->bhtv', scores, vi)
        b_last = b[..., -1:, :]
        new_state = state * jnp.exp(b_last)[..., 0, :, None] + jnp.einsum(
            'bhck,bhcv->bhkv', ki * jnp.exp(b_last - b), vi)
        return new_state, o_inter + o_intra

    state0 = jnp.zeros((bsz, h, dk, dv), jnp.float32)
    _, o = lax.scan(step, state0, (to_chunks(q), to_chunks(k), to_chunks(v), to_chunks(log_f)))
    return o.transpose(1, 0, 3, 2, 4).reshape(bsz, s, h, dv)


def _hgrn2_mixer(q_pre, fz_fwd, fz_bwd, i_pre, gate, lb_f, lb_b, norm_w):
    bsz, s, _ = q_pre.shape

    def heads(t):
        return t.reshape(bsz, s, HGRN_HEADS, HGRN_HEAD_DIM).astype(jnp.float32)

    q = jax.nn.silu(heads(q_pre))
    v = heads(i_pre)

    def gates(fz, lb):
        lb = lb.reshape(HGRN_HEADS, HGRN_HEAD_DIM)
        log_f = jnp.logaddexp(jnp.log(lb), jnp.log1p(-lb) + jax.nn.log_sigmoid(heads(fz)))
        return -jnp.expm1(log_f), log_f

    k_f, g_f = gates(fz_fwd, lb_f)
    k_b, g_b = gates(fz_bwd, lb_b)
    o_fwd = _chunk_recurrence(q, k_f, v, g_f)
    flip = lambda t: jnp.flip(t, axis=1)
    o_bwd = flip(_chunk_recurrence(flip(q), flip(k_b), flip(v), flip(g_b)))
    o = _rmsnorm(o_fwd + o_bwd, norm_w.reshape(HGRN_HEADS, HGRN_HEAD_DIM))
    o = o.reshape(bsz, s, D_HGRN) * jax.nn.silu(gate.astype(jnp.float32))
    return o.astype(q_pre.dtype)


def _short_conv_mixer(b_gate, c_gate, h, conv_w):
    u = c_gate * h
    rhs = conv_w[:, None, :].astype(u.dtype)
    y = lax.conv_general_dilated(u, rhs, window_strides=(1,), padding=((1, 1),),
                                 dimension_numbers=('NWC', 'WIO', 'NWC'),
                                 feature_group_count=D_CONV)
    return b_gate * y


def setup_inputs(seed: int = 0) -> dict:
    key = jax.random.key(seed)
    ks = jax.random.split(key, 12)
    f32 = jnp.float32
    nrm = lambda k, shape, fan_in: jax.random.normal(k, shape, f32) * (fan_in ** -0.5)
    return {
        "x": jax.random.normal(ks[0], (BATCH, SEQ, D_MODEL), f32),
        "attn_norm_w": 1.0 + 0.02 * jax.random.normal(ks[1], (DEPTH, D_MODEL), f32),
        "w_in": nrm(ks[2], (DEPTH, D_MODEL, D_IN_PROJ), D_MODEL),
        "lb_fwd": 0.1 * jax.random.normal(ks[3], (DEPTH, D_HGRN), f32),
        "lb_bwd": 0.1 * jax.random.normal(ks[4], (DEPTH, D_HGRN), f32),
        "hgrn_norm_w": 1.0 + 0.02 * jax.random.normal(ks[5], (DEPTH, D_HGRN), f32),
        "conv_w": nrm(ks[6], (DEPTH, CONV_WIDTH, D_CONV), CONV_WIDTH),
        "w_out": nrm(ks[7], (DEPTH, D_MIX, D_MODEL), D_MIX),
        "ffn_norm_w": 1.0 + 0.02 * jax.random.normal(ks[8], (DEPTH, D_MODEL), f32),
        "w_gate_up": nrm(ks[9], (DEPTH, D_MODEL, 2 * D_FF), D_MODEL),
        "w_down": nrm(ks[10], (DEPTH, D_FF, D_MODEL), D_FF),
        "final_norm_w": 1.0 + 0.02 * jax.random.normal(ks[11], (D_MODEL,), f32),
    }


def reference(x, attn_norm_w, w_in, lb_fwd, lb_bwd, hgrn_norm_w, conv_w, w_out,
              ffn_norm_w, w_gate_up, w_down, final_norm_w):
    lbs_f = _lower_bounds(lb_fwd)
    lbs_b = _lower_bounds(lb_bwd)
    for l in range(DEPTH):
        h = _rmsnorm(x, attn_norm_w[l])
        proj = jnp.einsum('bsd,de->bse', h, w_in[l])
        q_pre, fz_f, fz_b, i_pre, gate, b_gate, c_gate, h_conv = jnp.split(proj, SPLIT_OFFSETS, axis=-1)
        y_rec = _hgrn2_mixer(q_pre, fz_f, fz_b, i_pre, gate, lbs_f[l], lbs_b[l], hgrn_norm_w[l])
        y_conv = _short_conv_mixer(b_gate, c_gate, h_conv, conv_w[l])
        mixed = jnp.concatenate([y_rec, y_conv], axis=-1)
        x = x + jnp.einsum('bse,ed->bsd', mixed, w_out[l])
        h2 = _rmsnorm(x, ffn_norm_w[l])
        g, u = jnp.split(jnp.einsum('bsd,df->bsf', h2, w_gate_up[l]), 2, axis=-1)
        x = x + jnp.einsum('bsf,fd->bsd', jax.nn.silu(g) * u, w_down[l])
    return _rmsnorm(x, final_norm_w)
```

```python
import functools

import jax
import jax.numpy as jnp
from jax import lax
from jax.experimental import pallas as pl
from jax.experimental.pallas import tpu as pltpu

F32 = jnp.float32
BF16 = jnp.bfloat16

EPS = 1e-6
HEAD_DIM = 128
CHUNK = 128
SUBLANES = 8
CHUNKS_PER_BLOCK = SUBLANES
TOK_BLOCK = CHUNK * CHUNKS_PER_BLOCK
LEVELS = tuple(1 << b for b in range(CHUNK.bit_length() - 1))
N_LEVELS = len(LEVELS) + 1
CONV_WIDTH = 3
VMEM_LIMIT = 52 * 1024 * 1024

_NT = (((1,), (1,)), ((), ()))
_TN = (((0,), (0,)), ((), ()))


def _sigmoid(x):
    return 1.0 / (1.0 + jnp.exp(-x))


def _rms(x, w):
    ms = jnp.mean(x * x, axis=-1, keepdims=True)
    return x * lax.rsqrt(ms + EPS) * w


def _params(sem):
    return pltpu.CompilerParams(dimension_semantics=sem, vmem_limit_bytes=VMEM_LIMIT)


def _in_proj_kernel(x_ref, nw_ref, w_ref, o_ref, h_scr):
    @pl.when(pl.program_id(1) == 0)
    def _():
        h_scr[...] = _rms(x_ref[...], nw_ref[...]).astype(BF16)

    o_ref[...] = jnp.dot(h_scr[...], w_ref[...], preferred_element_type=F32)


def _in_proj(x, nw, w, *, tm, tn):
    s, d = x.shape
    n = w.shape[1]
    return pl.pallas_call(
        _in_proj_kernel,
        grid=(s // tm, n // tn),
        in_specs=[
            pl.BlockSpec((tm, d), lambda i, j: (i, 0)),
            pl.BlockSpec((1, d), lambda i, j: (0, 0)),
            pl.BlockSpec((d, tn), lambda i, j: (0, j)),
        ],
        out_specs=pl.BlockSpec((tm, tn), lambda i, j: (i, j)),
        out_shape=jax.ShapeDtypeStruct((s, n), F32),
        scratch_shapes=[pltpu.VMEM((tm, d), BF16)],
        compiler_params=_params(("parallel", "arbitrary")),
        name="in_proj",
    )(x, nw, w)


def _lower_bound(lb_ref, layer):
    p = lb_ref[...]
    if layer == 0:
        return jnp.zeros((1, p.shape[1]), F32)
    e = jnp.exp(p - jnp.max(p, axis=0, keepdims=True))
    sm = e / jnp.sum(e, axis=0, keepdims=True)
    return jnp.sum(sm[1:layer + 1], axis=0, keepdims=True)


def _gates(z, lb):
    t = jnp.exp(-jnp.abs(z))
    r = 1.0 / (1.0 + t)
    tr = t * r
    pos = z >= 0
    f = lb + (1.0 - lb) * jnp.where(pos, r, tr)
    k = (1.0 - lb) * jnp.where(pos, tr, r)
    return f, k


def _halves(x, h):
    x5 = x.reshape((CHUNK // (2 * h), 2, h) + x.shape[1:])
    return x5[:, 0], x5[:, 1]


def _join(a, b):
    x = jnp.stack([a, b], axis=1)
    return x.reshape((CHUNK,) + a.shape[2:])


def _chunk_rows(ref, r):
    return ref[pl.ds(r, CHUNK, stride=CHUNKS_PER_BLOCK), :]


def _rec_fwd_kernel(layer, lbf_ref, lbb_ref, q_ref, zf_ref, zb_ref, v_ref,
                    o_ref, qb_ref, kb_ref, tb_ref,
                    stq_ref, stk_ref, qf_ref, kf_ref, tf_ref, st_ref):
    @pl.when(pl.program_id(1) == 0)
    def _():
        st_ref[...] = jnp.zeros_like(st_ref)

    sh = (CHUNK, CHUNKS_PER_BLOCK, HEAD_DIM)
    flat = (TOK_BLOCK, HEAD_DIM)
    q = q_ref[...].reshape(sh)
    qs = q * _sigmoid(q)
    f_f, k_f = _gates(zf_ref[...].reshape(sh), _lower_bound(lbf_ref, layer))
    f_b, k_b = _gates(zb_ref[...].reshape(sh), _lower_bound(lbb_ref, layer))

    stq_ref[0] = qs.reshape(flat)
    stk_ref[0] = (k_f + k_b).reshape(flat)

    pin, sex, tot_f = f_f, jnp.ones(sh, F32), f_f
    sin, pex, tot_b = f_b, jnp.ones(sh, F32), f_b
    for li, h in enumerate(LEVELS):
        pin1, pin2 = _halves(pin, h)
        sex1, sex2 = _halves(sex, h)
        sin1, sin2 = _halves(sin, h)
        pex1, pex2 = _halves(pex, h)
        qs1, qs2 = _halves(qs, h)
        kf1, _ = _halves(k_f, h)
        _, kb2 = _halves(k_b, h)
        stq_ref[li + 1] = _join(qs1 * sin1, qs2 * pin2).reshape(flat)
        stk_ref[li + 1] = _join(kf1 * sex1, kb2 * pex2).reshape(flat)
        nb = CHUNK // (2 * h)
        tf4 = tot_f.reshape((nb, 2) + sh[1:])
        tb4 = tot_b.reshape((nb, 2) + sh[1:])
        pin = _join(pin1, pin2 * tf4[:, 0][:, None])
        sex = _join(sex1 * tf4[:, 1][:, None], sex2)
        sin = _join(sin1 * tb4[:, 1][:, None], sin2)
        pex = _join(pex1, pex2 * tb4[:, 0][:, None])
        tot_f = tf4[:, 0] * tf4[:, 1]
        tot_b = tb4[:, 0] * tb4[:, 1]

    qf_ref[...] = (qs * pin).reshape(flat)
    kf_ref[...] = (k_f * sex).reshape(flat)
    qb_ref[...] = (qs * sin).reshape(flat)
    kb_ref[...] = (k_b * pex).reshape(flat)
    tf_ref[...] = tot_f[0]
    tb_ref[...] = tot_b[0]

    row = lax.broadcasted_iota(jnp.int32, (CHUNK, CHUNK), 0)
    col = lax.broadcasted_iota(jnp.int32, (CHUNK, CHUNK), 1)
    xr = row ^ col
    lvl = jnp.zeros((CHUNK, CHUNK), jnp.int32)
    for h in LEVELS:
        lvl = lvl + jnp.where(xr >= h, 1, 0)

    for r in range(CHUNKS_PER_BLOCK):
        a = None
        for li in range(N_LEVELS):
            p = lax.dot_general(_chunk_rows(stq_ref.at[li], r).astype(BF16),
                                _chunk_rows(stk_ref.at[li], r).astype(BF16),
                                _NT, preferred_element_type=F32)
            a = p if a is None else jnp.where(lvl == li, p, a)
        vb = _chunk_rows(v_ref, r).astype(BF16)
        st = st_ref[...]
        o = jnp.dot(a.astype(BF16), vb, preferred_element_type=F32)
        o = o + lax.dot_general(_chunk_rows(qf_ref, r).astype(BF16), st.astype(BF16),
                                _NT, preferred_element_type=F32)
        upd = lax.dot_general(vb, _chunk_rows(kf_ref, r).astype(BF16), _TN,
                              preferred_element_type=F32)
        st_ref[...] = st * tf_ref[pl.ds(r, 1), :] + upd
        o_ref[pl.ds(r, CHUNK, stride=CHUNKS_PER_BLOCK), :] = o


def _rec_fwd(proj, lb_fwd, lb_bwd, layer, d_hgrn):
    s = proj.shape[0]
    depth = lb_fwd.shape[0]
    nh = d_hgrn // HEAD_DIM
    nblk = s // TOK_BLOCK
    blk = (TOK_BLOCK, HEAD_DIM)

    def col(group):
        return pl.BlockSpec(blk, lambda h, i, g=group: (i, g * nh + h))

    lb_spec = pl.BlockSpec((depth, HEAD_DIM), lambda h, i: (0, h))
    act = jax.ShapeDtypeStruct((s, d_hgrn), F32)
    return pl.pallas_call(
        functools.partial(_rec_fwd_kernel, layer),
        grid=(nh, nblk),
        in_specs=[lb_spec, lb_spec, col(0), col(1), col(2), col(3)],
        out_specs=[
            pl.BlockSpec(blk, lambda h, i: (i, h)),
            pl.BlockSpec(blk, lambda h, i: (i, h)),
            pl.BlockSpec(blk, lambda h, i: (i, h)),
            pl.BlockSpec((CHUNKS_PER_BLOCK, HEAD_DIM), lambda h, i: (i, h)),
        ],
        out_shape=[act, act, act,
                   jax.ShapeDtypeStruct((nblk * CHUNKS_PER_BLOCK, d_hgrn), F32)],
        scratch_shapes=[
            pltpu.VMEM((N_LEVELS,) + blk, F32),
            pltpu.VMEM((N_LEVELS,) + blk, F32),
            pltpu.VMEM(blk, F32),
            pltpu.VMEM(blk, F32),
            pltpu.VMEM((CHUNKS_PER_BLOCK, HEAD_DIM), F32),
            pltpu.VMEM((HEAD_DIM, HEAD_DIM), F32),
        ],
        compiler_params=_params(("parallel", "arbitrary")),
        name="rec_fwd",
    )(lb_fwd, lb_bwd, proj, proj, proj, proj)


def _rec_bwd_kernel(qb_ref, kb_ref, tb_ref, v_ref, op_ref, g_ref, nw_ref, y_ref, st_ref):
    @pl.when(pl.program_id(1) == 0)
    def _():
        st_ref[...] = jnp.zeros_like(st_ref)

    nw = nw_ref[...]
    for r in reversed(range(CHUNKS_PER_BLOCK)):
        st = st_ref[...]
        o = _chunk_rows(op_ref, r) + lax.dot_general(
            _chunk_rows(qb_ref, r).astype(BF16), st.astype(BF16), _NT,
            preferred_element_type=F32)
        upd = lax.dot_general(_chunk_rows(v_ref, r).astype(BF16),
                              _chunk_rows(kb_ref, r).astype(BF16), _TN,
                              preferred_element_type=F32)
        st_ref[...] = st * tb_ref[pl.ds(r, 1), :] + upd
        g = _chunk_rows(g_ref, r)
        y_ref[pl.ds(r, CHUNK, stride=CHUNKS_PER_BLOCK), :] = _rms(o, nw) * (g * _sigmoid(g))


def _rec_bwd(qb, kb, tb, proj, o_part, norm_w, d_hgrn):
    s = proj.shape[0]
    nh = d_hgrn // HEAD_DIM
    nblk = s // TOK_BLOCK
    blk = (TOK_BLOCK, HEAD_DIM)
    rev = lambda h, i: (nblk - 1 - i, h)

    def col(group):
        return pl.BlockSpec(blk, lambda h, i, g=group: (nblk - 1 - i, g * nh + h))

    return pl.pallas_call(
        _rec_bwd_kernel,
        grid=(nh, nblk),
        in_specs=[
            pl.BlockSpec(blk, rev),
            pl.BlockSpec(blk, rev),
            pl.BlockSpec((CHUNKS_PER_BLOCK, HEAD_DIM), rev),
            col(3),
            pl.BlockSpec(blk, rev),
            col(4),
            pl.BlockSpec((1, HEAD_DIM), lambda h, i: (0, h)),
        ],
        out_specs=pl.BlockSpec(blk, rev),
        out_shape=jax.ShapeDtypeStruct((s, d_hgrn), F32),
        scratch_shapes=[pltpu.VMEM((HEAD_DIM, HEAD_DIM), F32)],
        compiler_params=_params(("parallel", "arbitrary")),
        name="rec_bwd",
    )(qb, kb, tb, proj, o_part, proj, norm_w)


def _conv_kernel(b_ref, c_ref, h_ref, cp_ref, hp_ref, cn_ref, hn_ref, w_ref, y_ref):
    i = pl.program_id(0)
    last = pl.num_programs(0) - 1
    u = c_ref[...] * h_ref[...]
    n = u.shape[0]
    sub = lax.broadcasted_iota(jnp.int32, (SUBLANES, u.shape[1]), 0)
    halo_p = jnp.where(i > 0, cp_ref[...] * hp_ref[...], 0.0)
    first = jnp.where(sub == 0, pltpu.roll(halo_p, 1, 0), pltpu.roll(u[n - SUBLANES:], 1, 0))
    u_prev = jnp.concatenate([first, u[:n - SUBLANES]], axis=0)
    halo_n = jnp.where(i < last, cn_ref[...] * hn_ref[...], 0.0)
    lastv = jnp.where(sub == SUBLANES - 1, pltpu.roll(halo_n, SUBLANES - 1, 0),
                      pltpu.roll(u[:SUBLANES], SUBLANES - 1, 0))
    u_next = jnp.concatenate([u[SUBLANES:], lastv], axis=0)
    w = w_ref[...]
    y = w[0:1] * u_prev + w[1:2] * u + w[2:3] * u_next
    y_ref[...] = (b_ref[...] * y).astype(y_ref.dtype)


def _conv(proj, conv_w, d_hgrn, d_conv, *, tc):
    s = proj.shape[0]
    nblk = s // TOK_BLOCK
    nrow8 = s // SUBLANES
    per_blk = TOK_BLOCK // SUBLANES
    base = 5 * d_hgrn // tc
    ncol = d_conv // tc

    def main(group):
        return pl.BlockSpec((TOK_BLOCK, tc), lambda i, j, g=group: (i, base + g * ncol + j))

    def prev(group):
        return pl.BlockSpec((SUBLANES, tc),
                            lambda i, j, g=group: (jnp.maximum(i * per_blk - 1, 0), base + g * ncol + j))

    def nxt(group):
        return pl.BlockSpec((SUBLANES, tc),
                            lambda i, j, g=group: (jnp.minimum((i + 1) * per_blk, nrow8 - 1),
                                                   base + g * ncol + j))

    return pl.pallas_call(
        _conv_kernel,
        grid=(nblk, ncol),
        in_specs=[main(0), main(1), main(2), prev(1), prev(2), nxt(1), nxt(2),
                  pl.BlockSpec((CONV_WIDTH, tc), lambda i, j: (0, j))],
        out_specs=pl.BlockSpec((TOK_BLOCK, tc), lambda i, j: (i, j)),
        out_shape=jax.ShapeDtypeStruct((s, d_conv), BF16),
        compiler_params=_params(("parallel", "parallel")),
        name="short_conv",
    )(proj, proj, proj, proj, proj, proj, proj, conv_w)


def _out_proj_kernel(x_ref, yr_ref, yc_ref, wr_ref, wc_ref, o_ref):
    acc = jnp.dot(yr_ref[...].astype(BF16), wr_ref[...], preferred_element_type=F32)
    acc = acc + jnp.dot(yc_ref[...], wc_ref[...], preferred_element_type=F32)
    o_ref[...] = x_ref[...] + acc


def _out_proj(x, y_rec, y_conv, w, *, tm):
    s, d = x.shape
    d_rec = y_rec.shape[1]
    d_conv = y_conv.shape[1]
    assert d_rec == d_conv
    return pl.pallas_call(
        _out_proj_kernel,
        grid=(s // tm,),
        in_specs=[
            pl.BlockSpec((tm, d), lambda i: (i, 0)),
            pl.BlockSpec((tm, d_rec), lambda i: (i, 0)),
            pl.BlockSpec((tm, d_conv), lambda i: (i, 0)),
            pl.BlockSpec((d_rec, d), lambda i: (0, 0)),
            pl.BlockSpec((d_conv, d), lambda i: (1, 0)),
        ],
        out_specs=pl.BlockSpec((tm, d), lambda i: (i, 0)),
        out_shape=jax.ShapeDtypeStruct((s, d), F32),
        compiler_params=_params(("parallel",)),
        name="out_proj",
    )(x, y_rec, y_conv, w, w)


def _ffn_kernel(final, x_ref, nw_ref, wg_ref, wu_ref, wd_ref, fw_ref, o_ref, h_scr, acc_scr):
    f = pl.program_id(1)

    @pl.when(f == 0)
    def _():
        h_scr[...] = _rms(x_ref[...], nw_ref[...]).astype(BF16)
        acc_scr[...] = jnp.zeros_like(acc_scr)

    h = h_scr[...]
    g = jnp.dot(h, wg_ref[...], preferred_element_type=F32)
    u = jnp.dot(h, wu_ref[...], preferred_element_type=F32)
    a = (g * _sigmoid(g) * u).astype(BF16)
    acc_scr[...] += jnp.dot(a, wd_ref[...], preferred_element_type=F32)

    @pl.when(f == pl.num_programs(1) - 1)
    def _():
        y = x_ref[...] + acc_scr[...]
        if final:
            y = _rms(y, fw_ref[...])
        o_ref[...] = y


def _ffn(x, nw, w_gu, w_d, fw, final, *, tm, tf):
    s, d = x.shape
    d_ff = w_d.shape[0]
    nf = d_ff // tf
    return pl.pallas_call(
        functools.partial(_ffn_kernel, final),
        grid=(s // tm, nf),
        in_specs=[
            pl.BlockSpec((tm, d), lambda i, f: (i, 0)),
            pl.BlockSpec((1, d), lambda i, f: (0, 0)),
            pl.BlockSpec((d, tf), lambda i, f: (0, f)),
            pl.BlockSpec((d, tf), lambda i, f: (0, nf + f)),
            pl.BlockSpec((tf, d), lambda i, f: (f, 0)),
            pl.BlockSpec((1, d), lambda i, f: (0, 0)),
        ],
        out_specs=pl.BlockSpec((tm, d), lambda i, f: (i, 0)),
        out_shape=jax.ShapeDtypeStruct((s, d), F32),
        scratch_shapes=[pltpu.VMEM((tm, d), BF16), pltpu.VMEM((tm, d), F32)],
        compiler_params=_params(("parallel", "arbitrary")),
        name="ffn",
    )(x, nw, w_gu, w_gu, w_d, fw)


def _interleave(x):
    s, d = x.shape
    x4 = x.reshape(s // TOK_BLOCK, CHUNKS_PER_BLOCK, CHUNK, d)
    return x4.transpose(0, 2, 1, 3).reshape(s, d)


def _deinterleave(x):
    s, d = x.shape
    x4 = x.reshape(s // TOK_BLOCK, CHUNK, CHUNKS_PER_BLOCK, d)
    return x4.transpose(0, 2, 1, 3).reshape(s, d)


def kernel(x, attn_norm_w, w_in, lb_fwd, lb_bwd, hgrn_norm_w, conv_w, w_out, ffn_norm_w,
           w_gate_up, w_down, final_norm_w):
    bsz, s, d = x.shape
    depth = w_in.shape[0]
    d_hgrn = lb_fwd.shape[1]
    d_conv = conv_w.shape[2]
    assert bsz == 1 and s % TOK_BLOCK == 0
    assert w_in.shape[2] == 5 * d_hgrn + 3 * d_conv and d_hgrn == d_conv

    xs = _interleave(x[0])
    fw = final_norm_w.reshape(1, d)
    for l in range(depth):
        proj = _in_proj(xs, attn_norm_w[l].reshape(1, d), w_in[l].astype(BF16), tm=1024, tn=1024)
        o_part, qb, kb, tb = _rec_fwd(proj, lb_fwd, lb_bwd, l, d_hgrn)
        y_rec = _rec_bwd(qb, kb, tb, proj, o_part, hgrn_norm_w[l].reshape(1, d_hgrn), d_hgrn)
        y_conv = _conv(proj, conv_w[l], d_hgrn, d_conv, tc=512)
        xs = _out_proj(xs, y_rec, y_conv, w_out[l].astype(BF16), tm=512)
        xs = _ffn(xs, ffn_norm_w[l].reshape(1, d), w_gate_up[l].astype(BF16),
                  w_down[l].astype(BF16), fw, l == depth - 1, tm=512, tf=512)
    return _deinterleave(xs)[None]
```

```python
import functools

import jax
import jax.numpy as jnp
from jax import lax
from jax.experimental import pallas as pl
from jax.experimental.pallas import tpu as pltpu

F32 = jnp.float32
BF16 = jnp.bfloat16

EPS = 1e-6
HEAD_DIM = 128
CHUNK = 128
SUBLANES = 8
CHUNKS_PER_BLOCK = SUBLANES
TOK_BLOCK = CHUNK * CHUNKS_PER_BLOCK
POS_BLOCK = SUBLANES
N_GROUPS = CHUNK // POS_BLOCK
GROUP_ROWS = POS_BLOCK * CHUNKS_PER_BLOCK
LOW_LEVELS = tuple(1 << b for b in range(POS_BLOCK.bit_length() - 1))
HIGH_SPANS = tuple(1 << b for b in range(N_GROUPS.bit_length() - 1))
N_LOW = len(LOW_LEVELS) + 1
N_LEVELS = N_LOW + len(HIGH_SPANS)
N_FAC_LEVEL = 2 * (len(HIGH_SPANS) - 1)
FAC_QF, FAC_KF = N_FAC_LEVEL, N_FAC_LEVEL + 1
N_FAC = N_FAC_LEVEL + 2
FACB_Q, FACB_K = 0, 1
FAC_ROWS = N_FAC * N_GROUPS * CHUNKS_PER_BLOCK
FACB_ROWS = 2 * N_GROUPS * CHUNKS_PER_BLOCK
CONV_WIDTH = 3
VMEM_LIMIT = 52 * 1024 * 1024

_NT = (((1,), (1,)), ((), ()))
_TN = (((0,), (0,)), ((), ()))


def _sigmoid(x):
    return 1.0 / (1.0 + jnp.exp(-x))


def _rms(x, w):
    ms = jnp.mean(x * x, axis=-1, keepdims=True)
    return x * lax.rsqrt(ms + EPS) * w


def _params(sem):
    return pltpu.CompilerParams(dimension_semantics=sem, vmem_limit_bytes=VMEM_LIMIT)


def _in_proj_kernel(x_ref, nw_ref, w_ref, o_ref, h_scr):
    @pl.when(pl.program_id(1) == 0)
    def _():
        h_scr[...] = _rms(x_ref[...], nw_ref[...]).astype(BF16)

    o_ref[...] = jnp.dot(h_scr[...], w_ref[...], preferred_element_type=F32)


def _in_proj(x, nw, w, layer, *, tm, tn):
    s, d = x.shape
    n = w.shape[2]
    return pl.pallas_call(
        _in_proj_kernel,
        grid=(s // tm, n // tn),
        in_specs=[
            pl.BlockSpec((tm, d), lambda i, j: (i, 0)),
            pl.BlockSpec((1, d), lambda i, j: (0, 0)),
            pl.BlockSpec((None, d, tn), lambda i, j: (layer, 0, j)),
        ],
        out_specs=pl.BlockSpec((tm, tn), lambda i, j: (i, j)),
        out_shape=jax.ShapeDtypeStruct((s, n), F32),
        scratch_shapes=[pltpu.VMEM((tm, d), BF16)],
        compiler_params=_params(("parallel", "arbitrary")),
        name="in_proj",
    )(x, nw, w)


def _lower_bound(lb_ref, layer):
    p = lb_ref[...]
    if layer == 0:
        return jnp.zeros((1, p.shape[1]), F32)
    e = jnp.exp(p - jnp.max(p, axis=0, keepdims=True))
    sm = e / jnp.sum(e, axis=0, keepdims=True)
    return jnp.sum(sm[1:layer + 1], axis=0, keepdims=True)


def _gates(z, lb):
    t = jnp.exp(-jnp.abs(z))
    r = 1.0 / (1.0 + t)
    tr = t * r
    pos = z >= 0
    f = lb + (1.0 - lb) * jnp.where(pos, r, tr)
    k = (1.0 - lb) * jnp.where(pos, tr, r)
    return f, k


def _halves(x, h):
    x5 = x.reshape((x.shape[0] // (2 * h), 2, h) + x.shape[1:])
    return x5[:, 0], x5[:, 1]


def _join(a, b):
    x = jnp.stack([a, b], axis=1)
    return x.reshape((x.shape[0] * x.shape[1] * x.shape[2],) + a.shape[2:])


def _chunk_rows(ref, r):
    return ref[pl.ds(r, CHUNK, stride=CHUNKS_PER_BLOCK), :]


def _fac_rows(k, g):
    return pl.ds((k * N_GROUPS + g) * CHUNKS_PER_BLOCK, CHUNKS_PER_BLOCK)


def _group_factor(fac_ref, k, r):
    return jnp.concatenate(
        [jnp.broadcast_to(fac_ref[pl.ds((k * N_GROUPS + g) * CHUNKS_PER_BLOCK + r, 1), :],
                          (POS_BLOCK, HEAD_DIM))
         for g in range(N_GROUPS)], axis=0)


def _by_group_parity(even, odd, span):
    pieces = []
    for g in range(N_GROUPS):
        src = odd if (g // span) % 2 else even
        pieces.append(src[g * POS_BLOCK:(g + 1) * POS_BLOCK])
    return jnp.concatenate(pieces, axis=0)


def _seg_products(tot, span):
    ones = jnp.ones_like(tot[0])
    pre, suf = [None] * N_GROUPS, [None] * N_GROUPS
    for s0 in range(0, N_GROUPS, span):
        acc = None
        for g in range(s0, s0 + span):
            pre[g] = ones if acc is None else acc
            acc = tot[g] if acc is None else acc * tot[g]
        acc = None
        for g in reversed(range(s0, s0 + span)):
            suf[g] = ones if acc is None else acc
            acc = tot[g] if acc is None else acc * tot[g]
    return pre, suf


def _rec_fwd_kernel(layer, lbf_ref, lbb_ref, q_ref, zf_ref, zb_ref, v_ref,
                    o_ref, qsb_ref, kpb_ref, facb_ref, tb_ref,
                    stq_ref, stk_ref, qp_ref, ks_ref, fac_ref, tf_ref, st_ref):
    @pl.when(pl.program_id(1) == 0)
    def _():
        st_ref[...] = jnp.zeros_like(st_ref)

    lbf = _lower_bound(lbf_ref, layer)
    lbb = _lower_bound(lbb_ref, layer)
    sh = (POS_BLOCK, CHUNKS_PER_BLOCK, HEAD_DIM)
    flat = (GROUP_ROWS, HEAD_DIM)

    tot_f, tot_b = [], []
    for g in range(N_GROUPS):
        rows = pl.ds(g * GROUP_ROWS, GROUP_ROWS)
        q = q_ref[rows, :].reshape(sh)
        qs = q * _sigmoid(q)
        f_f, k_f = _gates(zf_ref[rows, :].reshape(sh), lbf)
        f_b, k_b = _gates(zb_ref[rows, :].reshape(sh), lbb)
        stq_ref[0, rows, :] = qs.reshape(flat)
        stk_ref[0, rows, :] = (k_f + k_b).reshape(flat)
        pin, sex, tf = f_f, jnp.ones(sh, F32), f_f
        sin, pex, tb = f_b, jnp.ones(sh, F32), f_b
        for li, h in enumerate(LOW_LEVELS):
            pin1, pin2 = _halves(pin, h)
            sex1, sex2 = _halves(sex, h)
            sin1, sin2 = _halves(sin, h)
            pex1, pex2 = _halves(pex, h)
            qs1, qs2 = _halves(qs, h)
            kf1, _ = _halves(k_f, h)
            _, kb2 = _halves(k_b, h)
            stq_ref[li + 1, rows, :] = _join(qs1 * sin1, qs2 * pin2).reshape(flat)
            stk_ref[li + 1, rows, :] = _join(kf1 * sex1, kb2 * pex2).reshape(flat)
            tf1, tf2 = _halves(tf, 1)
            tb1, tb2 = _halves(tb, 1)
            pin = _join(pin1, pin2 * tf1)
            sex = _join(sex1 * tf2, sex2)
            sin = _join(sin1 * tb2, sin2)
            pex = _join(pex1, pex2 * tb1)
            tf = (tf1 * tf2).reshape((tf.shape[0] // 2,) + sh[1:])
            tb = (tb1 * tb2).reshape((tb.shape[0] // 2,) + sh[1:])
        qp_ref[rows, :] = (qs * pin).reshape(flat)
        ks_ref[rows, :] = (k_f * sex).reshape(flat)
        qsb_ref[rows, :] = (qs * sin).reshape(flat)
        kpb_ref[rows, :] = (k_b * pex).reshape(flat)
        tot_f.append(tf[0])
        tot_b.append(tb[0])

    for si, span in enumerate(HIGH_SPANS[1:]):
        pre_f, suf_f = _seg_products(tot_f, span)
        pre_b, suf_b = _seg_products(tot_b, span)
        for g in range(N_GROUPS):
            second = (g // span) % 2
            fac_ref[_fac_rows(2 * si, g), :] = pre_f[g] if second else suf_b[g]
            fac_ref[_fac_rows(2 * si + 1, g), :] = pre_b[g] if second else suf_f[g]
    pre_f, suf_f = _seg_products(tot_f, N_GROUPS)
    pre_b, suf_b = _seg_products(tot_b, N_GROUPS)
    for g in range(N_GROUPS):
        fac_ref[_fac_rows(FAC_QF, g), :] = pre_f[g]
        fac_ref[_fac_rows(FAC_KF, g), :] = suf_f[g]
        facb_ref[_fac_rows(FACB_Q, g), :] = suf_b[g]
        facb_ref[_fac_rows(FACB_K, g), :] = pre_b[g]
    tf_ref[...] = pre_f[N_GROUPS - 1] * tot_f[N_GROUPS - 1]
    tb_ref[...] = pre_b[N_GROUPS - 1] * tot_b[N_GROUPS - 1]

    row = lax.broadcasted_iota(jnp.int32, (CHUNK, CHUNK), 0)
    col = lax.broadcasted_iota(jnp.int32, (CHUNK, CHUNK), 1)
    xr = row ^ col
    lvl = jnp.zeros((CHUNK, CHUNK), jnp.int32)
    for b in range(N_LEVELS - 1):
        lvl = lvl + jnp.where(xr >= (1 << b), 1, 0)

    for r in range(CHUNKS_PER_BLOCK):
        qp = _chunk_rows(qp_ref, r)
        qs_ = _chunk_rows(qsb_ref, r)
        ks = _chunk_rows(ks_ref, r)
        kp = _chunk_rows(kpb_ref, r)
        a = None
        for li in range(N_LEVELS):
            if li < N_LOW:
                ql = _chunk_rows(stq_ref.at[li], r)
                kl = _chunk_rows(stk_ref.at[li], r)
            else:
                si = li - N_LOW
                span = HIGH_SPANS[si]
                ql = _by_group_parity(qs_, qp, span)
                kl = _by_group_parity(ks, kp, span)
                if si > 0:
                    ql = ql * _group_factor(fac_ref, 2 * (si - 1), r)
                    kl = kl * _group_factor(fac_ref, 2 * (si - 1) + 1, r)
            p = lax.dot_general(ql.astype(BF16), kl.astype(BF16), _NT, preferred_element_type=F32)
            a = p if a is None else jnp.where(lvl == li, p, a)
        vb = _chunk_rows(v_ref, r).astype(BF16)
        st = st_ref[...]
        o = jnp.dot(a.astype(BF16), vb, preferred_element_type=F32)
        qf = (qp * _group_factor(fac_ref, FAC_QF, r)).astype(BF16)
        o = o + lax.dot_general(qf, st.astype(BF16), _NT, preferred_element_type=F32)
        kf = (ks * _group_factor(fac_ref, FAC_KF, r)).astype(BF16)
        upd = lax.dot_general(vb, kf, _TN, preferred_element_type=F32)
        st_ref[...] = st * tf_ref[pl.ds(r, 1), :] + upd
        o_ref[pl.ds(r, CHUNK, stride=CHUNKS_PER_BLOCK), :] = o


def _rec_fwd(proj, lb_fwd, lb_bwd, layer, d_hgrn):
    s = proj.shape[0]
    depth = lb_fwd.shape[0]
    nh = d_hgrn // HEAD_DIM
    nblk = s // TOK_BLOCK
    blk = (TOK_BLOCK, HEAD_DIM)

    def col(group):
        return pl.BlockSpec(blk, lambda h, i, g=group: (i, g * nh + h))

    lb_spec = pl.BlockSpec((depth, HEAD_DIM), lambda h, i: (0, h))
    act = jax.ShapeDtypeStruct((s, d_hgrn), F32)
    return pl.pallas_call(
        functools.partial(_rec_fwd_kernel, layer),
        grid=(nh, nblk),
        in_specs=[lb_spec, lb_spec, col(0), col(1), col(2), col(3)],
        out_specs=[
            pl.BlockSpec(blk, lambda h, i: (i, h)),
            pl.BlockSpec(blk, lambda h, i: (i, h)),
            pl.BlockSpec(blk, lambda h, i: (i, h)),
            pl.BlockSpec((FACB_ROWS, HEAD_DIM), lambda h, i: (i, h)),
            pl.BlockSpec((CHUNKS_PER_BLOCK, HEAD_DIM), lambda h, i: (i, h)),
        ],
        out_shape=[act, act, act,
                   jax.ShapeDtypeStruct((nblk * FACB_ROWS, d_hgrn), F32),
                   jax.ShapeDtypeStruct((nblk * CHUNKS_PER_BLOCK, d_hgrn), F32)],
        scratch_shapes=[
            pltpu.VMEM((N_LOW,) + blk, F32),
            pltpu.VMEM((N_LOW,) + blk, F32),
            pltpu.VMEM(blk, F32),
            pltpu.VMEM(blk, F32),
            pltpu.VMEM((FAC_ROWS, HEAD_DIM), F32),
            pltpu.VMEM((CHUNKS_PER_BLOCK, HEAD_DIM), F32),
            pltpu.VMEM((HEAD_DIM, HEAD_DIM), F32),
        ],
        compiler_params=_params(("parallel", "arbitrary")),
        name="rec_fwd",
    )(lb_fwd, lb_bwd, proj, proj, proj, proj)


def _rec_bwd_kernel(qsb_ref, kpb_ref, facb_ref, tb_ref, v_ref, op_ref, g_ref, nw_ref,
                    y_ref, ob_ref, st_ref):
    @pl.when(pl.program_id(1) == 0)
    def _():
        st_ref[...] = jnp.zeros_like(st_ref)

    for r in reversed(range(CHUNKS_PER_BLOCK)):
        st = st_ref[...]
        qb = (_chunk_rows(qsb_ref, r) * _group_factor(facb_ref, FACB_Q, r)).astype(BF16)
        kb = (_chunk_rows(kpb_ref, r) * _group_factor(facb_ref, FACB_K, r)).astype(BF16)
        ob_ref[pl.ds(r, CHUNK, stride=CHUNKS_PER_BLOCK), :] = lax.dot_general(
            qb, st.astype(BF16), _NT, preferred_element_type=F32)
        upd = lax.dot_general(_chunk_rows(v_ref, r).astype(BF16), kb, _TN,
                              preferred_element_type=F32)
        st_ref[...] = st * tb_ref[pl.ds(r, 1), :] + upd
    g = g_ref[...]
    y = _rms(op_ref[...] + ob_ref[...], nw_ref[...]) * (g * _sigmoid(g))
    y_ref[...] = y.astype(y_ref.dtype)


def _rec_bwd(qsb, kpb, facb, tb, proj, o_part, norm_w, d_hgrn):
    s = proj.shape[0]
    nh = d_hgrn // HEAD_DIM
    nblk = s // TOK_BLOCK
    blk = (TOK_BLOCK, HEAD_DIM)
    rev = lambda h, i: (nblk - 1 - i, h)

    def col(group):
        return pl.BlockSpec(blk, lambda h, i, g=group: (nblk - 1 - i, g * nh + h))

    return pl.pallas_call(
        _rec_bwd_kernel,
        grid=(nh, nblk),
        in_specs=[
            pl.BlockSpec(blk, rev),
            pl.BlockSpec(blk, rev),
            pl.BlockSpec((FACB_ROWS, HEAD_DIM), rev),
            pl.BlockSpec((CHUNKS_PER_BLOCK, HEAD_DIM), rev),
            col(3),
            pl.BlockSpec(blk, rev),
            col(4),
            pl.BlockSpec((1, HEAD_DIM), lambda h, i: (0, h)),
        ],
        out_specs=pl.BlockSpec(blk, rev),
        out_shape=jax.ShapeDtypeStruct((s, d_hgrn), BF16),
        scratch_shapes=[pltpu.VMEM(blk, F32), pltpu.VMEM((HEAD_DIM, HEAD_DIM), F32)],
        compiler_params=_params(("parallel", "arbitrary")),
        name="rec_bwd",
    )(qsb, kpb, facb, tb, proj, o_part, proj, norm_w)


def _conv_kernel(b_ref, c_ref, h_ref, cp_ref, hp_ref, cn_ref, hn_ref, w_ref, y_ref):
    i = pl.program_id(0)
    last = pl.num_programs(0) - 1
    u = c_ref[...] * h_ref[...]
    n = u.shape[0]
    sub = lax.broadcasted_iota(jnp.int32, (SUBLANES, u.shape[1]), 0)
    halo_p = jnp.where(i > 0, cp_ref[...] * hp_ref[...], 0.0)
    first = jnp.where(sub == 0, pltpu.roll(halo_p, 1, 0), pltpu.roll(u[n - SUBLANES:], 1, 0))
    u_prev = jnp.concatenate([first, u[:n - SUBLANES]], axis=0)
    halo_n = jnp.where(i < last, cn_ref[...] * hn_ref[...], 0.0)
    lastv = jnp.where(sub == SUBLANES - 1, pltpu.roll(halo_n, SUBLANES - 1, 0),
                      pltpu.roll(u[:SUBLANES], SUBLANES - 1, 0))
    u_next = jnp.concatenate([u[SUBLANES:], lastv], axis=0)
    w = w_ref[...]
    y = w[0:1] * u_prev + w[1:2] * u + w[2:3] * u_next
    y_ref[...] = (b_ref[...] * y).astype(y_ref.dtype)


def _conv(proj, conv_w, d_hgrn, d_conv, *, tc):
    s = proj.shape[0]
    nblk = s // TOK_BLOCK
    nrow8 = s // SUBLANES
    per_blk = TOK_BLOCK // SUBLANES
    base = 5 * d_hgrn // tc
    ncol = d_conv // tc

    def main(group):
        return pl.BlockSpec((TOK_BLOCK, tc), lambda i, j, g=group: (i, base + g * ncol + j))

    def prev(group):
        return pl.BlockSpec((SUBLANES, tc),
                            lambda i, j, g=group: (jnp.maximum(i * per_blk - 1, 0), base + g * ncol + j))

    def nxt(group):
        return pl.BlockSpec((SUBLANES, tc),
                            lambda i, j, g=group: (jnp.minimum((i + 1) * per_blk, nrow8 - 1),
                                                   base + g * ncol + j))

    return pl.pallas_call(
        _conv_kernel,
        grid=(nblk, ncol),
        in_specs=[main(0), main(1), main(2), prev(1), prev(2), nxt(1), nxt(2),
                  pl.BlockSpec((CONV_WIDTH, tc), lambda i, j: (0, j))],
        out_specs=pl.BlockSpec((TOK_BLOCK, tc), lambda i, j: (i, j)),
        out_shape=jax.ShapeDtypeStruct((s, d_conv), BF16),
        compiler_params=_params(("parallel", "parallel")),
        name="short_conv",
    )(proj, proj, proj, proj, proj, proj, proj, conv_w)


def _out_proj_kernel(x_ref, yr_ref, yc_ref, wr_ref, wc_ref, o_ref):
    acc = jnp.dot(yr_ref[...], wr_ref[...], preferred_element_type=F32)
    acc = acc + jnp.dot(yc_ref[...], wc_ref[...], preferred_element_type=F32)
    o_ref[...] = x_ref[...] + acc


def _out_proj(x, y_rec, y_conv, w, layer, *, tm):
    s, d = x.shape
    d_rec = y_rec.shape[1]
    d_conv = y_conv.shape[1]
    assert d_rec == d_conv
    return pl.pallas_call(
        _out_proj_kernel,
        grid=(s // tm,),
        in_specs=[
            pl.BlockSpec((tm, d), lambda i: (i, 0)),
            pl.BlockSpec((tm, d_rec), lambda i: (i, 0)),
            pl.BlockSpec((tm, d_conv), lambda i: (i, 0)),
            pl.BlockSpec((None, d_rec, d), lambda i: (layer, 0, 0)),
            pl.BlockSpec((None, d_conv, d), lambda i: (layer, 1, 0)),
        ],
        out_specs=pl.BlockSpec((tm, d), lambda i: (i, 0)),
        out_shape=jax.ShapeDtypeStruct((s, d), F32),
        compiler_params=_params(("parallel",)),
        name="out_proj",
    )(x, y_rec, y_conv, w, w)


def _ffn_kernel(final, x_ref, nw_ref, wg_ref, wu_ref, wd_ref, fw_ref, o_ref, h_scr, acc_scr):
    f = pl.program_id(1)

    @pl.when(f == 0)
    def _():
        h_scr[...] = _rms(x_ref[...], nw_ref[...]).astype(BF16)
        acc_scr[...] = jnp.zeros_like(acc_scr)

    h = h_scr[...]
    g = jnp.dot(h, wg_ref[...], preferred_element_type=F32)
    u = jnp.dot(h, wu_ref[...], preferred_element_type=F32)
    a = (g * _sigmoid(g) * u).astype(BF16)
    acc_scr[...] += jnp.dot(a, wd_ref[...], preferred_element_type=F32)

    @pl.when(f == pl.num_programs(1) - 1)
    def _():
        y = x_ref[...] + acc_scr[...]
        if final:
            y = _rms(y, fw_ref[...])
        o_ref[...] = y


def _ffn(x, nw, w_gu, w_d, fw, layer, final, *, tm, tf):
    s, d = x.shape
    d_ff = w_d.shape[1]
    nf = d_ff // tf
    return pl.pallas_call(
        functools.partial(_ffn_kernel, final),
        grid=(s // tm, nf),
        in_specs=[
            pl.BlockSpec((tm, d), lambda i, f: (i, 0)),
            pl.BlockSpec((1, d), lambda i, f: (0, 0)),
            pl.BlockSpec((None, d, tf), lambda i, f: (layer, 0, f)),
            pl.BlockSpec((None, d, tf), lambda i, f: (layer, 0, nf + f)),
            pl.BlockSpec((None, tf, d), lambda i, f: (layer, f, 0)),
            pl.BlockSpec((1, d), lambda i, f: (0, 0)),
        ],
        out_specs=pl.BlockSpec((tm, d), lambda i, f: (i, 0)),
        out_shape=jax.ShapeDtypeStruct((s, d), F32),
        scratch_shapes=[pltpu.VMEM((tm, d), BF16), pltpu.VMEM((tm, d), F32)],
        compiler_params=_params(("parallel", "arbitrary")),
        name="ffn",
    )(x, nw, w_gu, w_gu, w_d, fw)


def _interleave(x):
    s, d = x.shape
    x4 = x.reshape(s // TOK_BLOCK, CHUNKS_PER_BLOCK, CHUNK, d)
    return x4.transpose(0, 2, 1, 3).reshape(s, d)


def _deinterleave(x):
    s, d = x.shape
    x4 = x.reshape(s // TOK_BLOCK, CHUNK, CHUNKS_PER_BLOCK, d)
    return x4.transpose(0, 2, 1, 3).reshape(s, d)


def kernel(x, attn_norm_w, w_in, lb_fwd, lb_bwd, hgrn_norm_w, conv_w, w_out, ffn_norm_w,
           w_gate_up, w_down, final_norm_w):
    bsz, s, d = x.shape
    depth = w_in.shape[0]
    d_hgrn = lb_fwd.shape[1]
    d_conv = conv_w.shape[2]
    assert bsz == 1 and s % TOK_BLOCK == 0
    assert w_in.shape[2] == 5 * d_hgrn + 3 * d_conv and d_hgrn == d_conv

    w_in, w_out, w_gate_up, w_down = (w.astype(BF16) for w in (w_in, w_out, w_gate_up, w_down))
    xs = _interleave(x[0])
    fw = final_norm_w.reshape(1, d)
    for l in range(depth):
        proj = _in_proj(xs, attn_norm_w[l].reshape(1, d), w_in, l, tm=1024, tn=1024)
        o_part, qsb, kpb, facb, tb = _rec_fwd(proj, lb_fwd, lb_bwd, l, d_hgrn)
        y_rec = _rec_bwd(qsb, kpb, facb, tb, proj, o_part, hgrn_norm_w[l].reshape(1, d_hgrn), d_hgrn)
        y_conv = _conv(proj, conv_w[l], d_hgrn, d_conv, tc=512)
        xs = _out_proj(xs, y_rec, y_conv, w_out, l, tm=512)
        xs = _ffn(xs, ffn_norm_w[l].reshape(1, d), w_gate_up, w_down, fw, l, l == depth - 1,
                  tm=512, tf=512)
    return _deinterleave(xs)[None]
```

```python
import functools

import jax
import jax.numpy as jnp
from jax import lax
from jax.experimental import pallas as pl
from jax.experimental.pallas import tpu as pltpu

F32 = jnp.float32
BF16 = jnp.bfloat16

EPS = 1e-6
HEAD_DIM = 128
CHUNK = 128
SUBLANES = 8
LANES = 128
CHUNKS_PER_BLOCK = SUBLANES
TOK_BLOCK = CHUNK * CHUNKS_PER_BLOCK
POS_BLOCK = SUBLANES
N_GROUPS = CHUNK // POS_BLOCK
GROUP_ROWS = POS_BLOCK * CHUNKS_PER_BLOCK
LOW_LEVELS = tuple(1 << b for b in range(POS_BLOCK.bit_length() - 1))
HIGH_SPANS = tuple(1 << b for b in range(N_GROUPS.bit_length() - 1))
N_LOW = len(LOW_LEVELS) + 1
N_LEVELS = N_LOW + len(HIGH_SPANS)
N_FAC_LEVEL = 2 * (len(HIGH_SPANS) - 1)
FAC_QF, FAC_KF = N_FAC_LEVEL, N_FAC_LEVEL + 1
N_FAC = N_FAC_LEVEL + 2
FACB_Q, FACB_K = 0, 1
FAC_ROWS = N_FAC * N_GROUPS * CHUNKS_PER_BLOCK
FACB_ROWS = 2 * N_GROUPS * CHUNKS_PER_BLOCK
BASE_QP, BASE_KS, BASE_QS, BASE_KP = range(4)
N_BASE = 4
N_STAGE = 5
CONV_WIDTH = 3
VMEM_LIMIT = 52 * 1024 * 1024
CAST_BLOCK_BYTES = 4 * 1024 * 1024

_NT = (((1,), (1,)), ((), ()))
_TN = (((0,), (0,)), ((), ()))


def _sigmoid(x):
    return 1.0 / (1.0 + jnp.exp(-x))


def _rms(x, w):
    ms = jnp.mean(x * x, axis=-1, keepdims=True)
    return x * lax.rsqrt(ms + EPS) * w


def _params(sem):
    return pltpu.CompilerParams(dimension_semantics=sem, vmem_limit_bytes=VMEM_LIMIT)


def _cast_kernel(w_ref, o_ref):
    o_ref[...] = w_ref[...].astype(o_ref.dtype)


def _cast_bf16(w):
    depth, r, c = w.shape
    rows = depth * r
    tr = 1 << ((CAST_BLOCK_BYTES // (4 * c)).bit_length() - 1)
    assert tr >= 2 * SUBLANES and rows % tr == 0
    out = pl.pallas_call(
        _cast_kernel,
        grid=(rows // tr,),
        in_specs=[pl.BlockSpec((tr, c), lambda i: (i, 0))],
        out_specs=pl.BlockSpec((tr, c), lambda i: (i, 0)),
        out_shape=jax.ShapeDtypeStruct((rows, c), BF16),
        compiler_params=_params(("parallel",)),
        name="cast_bf16",
    )(w.reshape(rows, c))
    return out.reshape(depth, r, c)


def _in_proj_kernel(x_ref, nw_ref, w_ref, o_ref, h_scr):
    @pl.when(pl.program_id(1) == 0)
    def _():
        h_scr[...] = _rms(x_ref[...], nw_ref[...]).astype(BF16)

    res = jnp.dot(h_scr[...], w_ref[...], preferred_element_type=F32)
    for c in range(o_ref.shape[0]):
        o_ref[c] = res[:, c * LANES:(c + 1) * LANES]


def _in_proj(x, nw, w, layer, *, tm, tn):
    s, d = x.shape
    n = w.shape[2]
    return pl.pallas_call(
        _in_proj_kernel,
        grid=(s // tm, n // tn),
        in_specs=[
            pl.BlockSpec((tm, d), lambda i, j: (i, 0)),
            pl.BlockSpec((1, d), lambda i, j: (0, 0)),
            pl.BlockSpec((None, d, tn), lambda i, j: (layer, 0, j)),
        ],
        out_specs=pl.BlockSpec((tn // LANES, tm, LANES), lambda i, j: (j, i, 0)),
        out_shape=jax.ShapeDtypeStruct((n // LANES, s, LANES), F32),
        scratch_shapes=[pltpu.VMEM((tm, d), BF16)],
        compiler_params=_params(("parallel", "arbitrary")),
        name="in_proj",
    )(x, nw, w)


def _lower_bound(lb_ref, layer):
    p = lb_ref[...]
    if layer == 0:
        return jnp.zeros((1, p.shape[1]), F32)
    e = jnp.exp(p - jnp.max(p, axis=0, keepdims=True))
    sm = e / jnp.sum(e, axis=0, keepdims=True)
    return jnp.sum(sm[1:layer + 1], axis=0, keepdims=True)


def _gates(z, lb):
    t = jnp.exp(-jnp.abs(z))
    r = 1.0 / (1.0 + t)
    tr = t * r
    pos = z >= 0
    f = lb + (1.0 - lb) * jnp.where(pos, r, tr)
    k = (1.0 - lb) * jnp.where(pos, tr, r)
    return f, k


def _halves(x, h):
    x5 = x.reshape((x.shape[0] // (2 * h), 2, h) + x.shape[1:])
    return x5[:, 0], x5[:, 1]


def _join(a, b):
    x = jnp.stack([a, b], axis=1)
    return x.reshape((x.shape[0] * x.shape[1] * x.shape[2],) + a.shape[2:])


def _chunk_rows(ref, r):
    return ref[pl.ds(r, CHUNK, stride=CHUNKS_PER_BLOCK), :]


def _fac_rows(k, g):
    return pl.ds((k * N_GROUPS + g) * CHUNKS_PER_BLOCK, CHUNKS_PER_BLOCK)


def _group_factor(fac_ref, k, r):
    return jnp.concatenate(
        [jnp.broadcast_to(fac_ref[pl.ds((k * N_GROUPS + g) * CHUNKS_PER_BLOCK + r, 1), :],
                          (POS_BLOCK, HEAD_DIM))
         for g in range(N_GROUPS)], axis=0)


def _by_group_parity(even, odd, span):
    pieces = []
    for g in range(N_GROUPS):
        src = odd if (g // span) % 2 else even
        pieces.append(src[g * POS_BLOCK:(g + 1) * POS_BLOCK])
    return jnp.concatenate(pieces, axis=0)


def _seg_products(tot, span):
    ones = jnp.ones_like(tot[0])
    pre, suf = [None] * N_GROUPS, [None] * N_GROUPS
    for s0 in range(0, N_GROUPS, span):
        acc = None
        for g in range(s0, s0 + span):
            pre[g] = ones if acc is None else acc
            acc = tot[g] if acc is None else acc * tot[g]
        acc = None
        for g in reversed(range(s0, s0 + span)):
            suf[g] = ones if acc is None else acc
            acc = tot[g] if acc is None else acc * tot[g]
    return pre, suf


def _scan_phase(layer, lbf_ref, lbb_ref, q_ref, zf_ref, zb_ref,
                qsb_ref, kpb_ref, facb_ref, tb_ref, stage):
    stq_ref, stk_ref, base_ref, fac_ref, tf_ref = stage
    lbf = _lower_bound(lbf_ref, layer)
    lbb = _lower_bound(lbb_ref, layer)
    sh = (POS_BLOCK, CHUNKS_PER_BLOCK, HEAD_DIM)
    flat = (GROUP_ROWS, HEAD_DIM)

    tot_f, tot_b = [], []
    for g in range(N_GROUPS):
        rows = pl.ds(g * GROUP_ROWS, GROUP_ROWS)
        q = q_ref[rows, :].reshape(sh)
        qs = q * _sigmoid(q)
        f_f, k_f = _gates(zf_ref[rows, :].reshape(sh), lbf)
        f_b, k_b = _gates(zb_ref[rows, :].reshape(sh), lbb)
        stq_ref[0, rows, :] = qs.reshape(flat)
        stk_ref[0, rows, :] = (k_f + k_b).reshape(flat)
        pin, sex, tf = f_f, jnp.ones(sh, F32), f_f
        sin, pex, tb = f_b, jnp.ones(sh, F32), f_b
        for li, h in enumerate(LOW_LEVELS):
            pin1, pin2 = _halves(pin, h)
            sex1, sex2 = _halves(sex, h)
            sin1, sin2 = _halves(sin, h)
            pex1, pex2 = _halves(pex, h)
            qs1, qs2 = _halves(qs, h)
            kf1, _ = _halves(k_f, h)
            _, kb2 = _halves(k_b, h)
            stq_ref[li + 1, rows, :] = _join(qs1 * sin1, qs2 * pin2).reshape(flat)
            stk_ref[li + 1, rows, :] = _join(kf1 * sex1, kb2 * pex2).reshape(flat)
            tf1, tf2 = _halves(tf, 1)
            tb1, tb2 = _halves(tb, 1)
            pin = _join(pin1, pin2 * tf1)
            sex = _join(sex1 * tf2, sex2)
            sin = _join(sin1 * tb2, sin2)
            pex = _join(pex1, pex2 * tb1)
            tf = (tf1 * tf2).reshape((tf.shape[0] // 2,) + sh[1:])
            tb = (tb1 * tb2).reshape((tb.shape[0] // 2,) + sh[1:])
        q_s = (qs * sin).reshape(flat)
        k_p = (k_b * pex).reshape(flat)
        base_ref[BASE_QP, rows, :] = (qs * pin).reshape(flat)
        base_ref[BASE_KS, rows, :] = (k_f * sex).reshape(flat)
        base_ref[BASE_QS, rows, :] = q_s
        base_ref[BASE_KP, rows, :] = k_p
        qsb_ref[rows, :] = q_s
        kpb_ref[rows, :] = k_p
        tot_f.append(tf[0])
        tot_b.append(tb[0])
        yield

    for si, span in enumerate(HIGH_SPANS[1:]):
        pre_f, suf_f = _seg_products(tot_f, span)
        pre_b, suf_b = _seg_products(tot_b, span)
        for g in range(N_GROUPS):
            second = (g // span) % 2
            fac_ref[_fac_rows(2 * si, g), :] = pre_f[g] if second else suf_b[g]
            fac_ref[_fac_rows(2 * si + 1, g), :] = pre_b[g] if second else suf_f[g]
    pre_f, suf_f = _seg_products(tot_f, N_GROUPS)
    pre_b, suf_b = _seg_products(tot_b, N_GROUPS)
    for g in range(N_GROUPS):
        fac_ref[_fac_rows(FAC_QF, g), :] = pre_f[g]
        fac_ref[_fac_rows(FAC_KF, g), :] = suf_f[g]
        facb_ref[_fac_rows(FACB_Q, g), :] = suf_b[g]
        facb_ref[_fac_rows(FACB_K, g), :] = pre_b[g]
    tf_ref[...] = pre_f[N_GROUPS - 1] * tot_f[N_GROUPS - 1]
    tb_ref[...] = pre_b[N_GROUPS - 1] * tot_b[N_GROUPS - 1]


def _matmul_phase(v_ref, o_ref, st_ref, stage):
    stq_ref, stk_ref, base_ref, fac_ref, tf_ref = stage
    row = lax.broadcasted_iota(jnp.int32, (CHUNK, CHUNK), 0)
    col = lax.broadcasted_iota(jnp.int32, (CHUNK, CHUNK), 1)
    xr = row ^ col
    lvl = jnp.zeros((CHUNK, CHUNK), jnp.int32)
    for b in range(N_LEVELS - 1):
        lvl = lvl + jnp.where(xr >= (1 << b), 1, 0)

    for r in range(CHUNKS_PER_BLOCK):
        qp = _chunk_rows(base_ref.at[BASE_QP], r)
        ks = _chunk_rows(base_ref.at[BASE_KS], r)
        qs = _chunk_rows(base_ref.at[BASE_QS], r)
        kp = _chunk_rows(base_ref.at[BASE_KP], r)
        a = None
        for li in range(N_LEVELS):
            if li < N_LOW:
                ql = _chunk_rows(stq_ref.at[li], r)
                kl = _chunk_rows(stk_ref.at[li], r)
            else:
                si = li - N_LOW
                span = HIGH_SPANS[si]
                ql = _by_group_parity(qs, qp, span)
                kl = _by_group_parity(ks, kp, span)
                if si > 0:
                    ql = ql * _group_factor(fac_ref, 2 * (si - 1), r)
                    kl = kl * _group_factor(fac_ref, 2 * (si - 1) + 1, r)
            p = lax.dot_general(ql.astype(BF16), kl.astype(BF16), _NT, preferred_element_type=F32)
            a = p if a is None else jnp.where(lvl == li, p, a)
        vb = _chunk_rows(v_ref, r).astype(BF16)
        st = st_ref[...]
        o = jnp.dot(a.astype(BF16), vb, preferred_element_type=F32)
        qf = (qp * _group_factor(fac_ref, FAC_QF, r)).astype(BF16)
        o = o + lax.dot_general(qf, st.astype(BF16), _NT, preferred_element_type=F32)
        kf = (ks * _group_factor(fac_ref, FAC_KF, r)).astype(BF16)
        upd = lax.dot_general(vb, kf, _TN, preferred_element_type=F32)
        st_ref[...] = st * tf_ref[pl.ds(r, 1), :] + upd
        o_ref[pl.ds(r, CHUNK, stride=CHUNKS_PER_BLOCK), :] = o
        yield


def _rec_fwd_kernel(layer, lbf_ref, lbb_ref, q_ref, zf_ref, zb_ref, v_ref,
                    o_ref, qsb_ref, kpb_ref, facb_ref, tb_ref, *scratch):
    st_ref = scratch[-1]
    slots = (scratch[0:N_STAGE], scratch[N_STAGE:2 * N_STAGE])
    i = pl.program_id(1)

    @pl.when(i == 0)
    def _():
        st_ref[...] = jnp.zeros_like(st_ref)
        for ref in slots[1]:
            ref[...] = jnp.zeros_like(ref)

    for parity in range(2):
        @pl.when(i % 2 == parity)
        def _(parity=parity):
            scan = _scan_phase(layer, lbf_ref, lbb_ref, q_ref, zf_ref, zb_ref,
                               qsb_ref, kpb_ref, facb_ref, tb_ref, slots[parity])
            matmuls = _matmul_phase(v_ref, o_ref, st_ref, slots[1 - parity])
            for _ in range(CHUNKS_PER_BLOCK):
                for _ in range(N_GROUPS // CHUNKS_PER_BLOCK):
                    next(scan)
                next(matmuls)
            for _ in scan:
                pass
            for _ in matmuls:
                pass


def _rec_fwd(proj, lb_fwd, lb_bwd, layer, d_hgrn):
    s = proj.shape[1]
    depth = lb_fwd.shape[0]
    nh = d_hgrn // HEAD_DIM
    nblk = s // TOK_BLOCK
    blk = (TOK_BLOCK, HEAD_DIM)
    hblk = (None,) + blk
    scanned = lambda h, i: (h, jnp.minimum(i, nblk - 1), 0)
    multiplied = lambda h, i: (h, jnp.maximum(i - 1, 0), 0)

    def col(group, shift):
        if shift:
            return pl.BlockSpec(hblk, lambda h, i, g=group: (g * nh + h, jnp.maximum(i - 1, 0), 0))
        return pl.BlockSpec(hblk, lambda h, i, g=group: (g * nh + h, jnp.minimum(i, nblk - 1), 0))

    lb_spec = pl.BlockSpec((depth, HEAD_DIM), lambda h, i: (0, h))
    act = jax.ShapeDtypeStruct((nh, s, HEAD_DIM), F32)
    stage = [
        pltpu.VMEM((N_LOW,) + blk, F32),
        pltpu.VMEM((N_LOW,) + blk, F32),
        pltpu.VMEM((N_BASE,) + blk, F32),
        pltpu.VMEM((FAC_ROWS, HEAD_DIM), F32),
        pltpu.VMEM((CHUNKS_PER_BLOCK, HEAD_DIM), F32),
    ]
    assert len(stage) == N_STAGE
    return pl.pallas_call(
        functools.partial(_rec_fwd_kernel, layer),
        grid=(nh, nblk + 1),
        in_specs=[lb_spec, lb_spec, col(0, False), col(1, False), col(2, False), col(3, True)],
        out_specs=[
            pl.BlockSpec(hblk, multiplied),
            pl.BlockSpec(hblk, scanned),
            pl.BlockSpec(hblk, scanned),
            pl.BlockSpec((None, FACB_ROWS, HEAD_DIM), scanned),
            pl.BlockSpec((None, CHUNKS_PER_BLOCK, HEAD_DIM), scanned),
        ],
        out_shape=[act, act, act,
                   jax.ShapeDtypeStruct((nh, nblk * FACB_ROWS, HEAD_DIM), F32),
                   jax.ShapeDtypeStruct((nh, nblk * CHUNKS_PER_BLOCK, HEAD_DIM), F32)],
        scratch_shapes=stage + stage + [pltpu.VMEM((HEAD_DIM, HEAD_DIM), F32)],
        compiler_params=_params(("parallel", "arbitrary")),
        name="rec_fwd",
    )(lb_fwd, lb_bwd, proj, proj, proj, proj)


def _rec_bwd_kernel(qsb_ref, kpb_ref, facb_ref, tb_ref, v_ref, op_ref, g_ref, nw_ref,
                    y_ref, ob_ref, st_ref):
    @pl.when(pl.program_id(1) == 0)
    def _():
        st_ref[...] = jnp.zeros_like(st_ref)

    for r in reversed(range(CHUNKS_PER_BLOCK)):
        st = st_ref[...]
        qb = (_chunk_rows(qsb_ref, r) * _group_factor(facb_ref, FACB_Q, r)).astype(BF16)
        kb = (_chunk_rows(kpb_ref, r) * _group_factor(facb_ref, FACB_K, r)).astype(BF16)
        ob_ref[pl.ds(r, CHUNK, stride=CHUNKS_PER_BLOCK), :] = lax.dot_general(
            qb, st.astype(BF16), _NT, preferred_element_type=F32)
        upd = lax.dot_general(_chunk_rows(v_ref, r).astype(BF16), kb, _TN,
                              preferred_element_type=F32)
        st_ref[...] = st * tb_ref[pl.ds(r, 1), :] + upd
    g = g_ref[...]
    y = _rms(op_ref[...] + ob_ref[...], nw_ref[...]) * (g * _sigmoid(g))
    y_ref[...] = y.astype(y_ref.dtype)


def _rec_bwd(qsb, kpb, facb, tb, proj, o_part, norm_w, d_hgrn):
    s = proj.shape[1]
    nh = d_hgrn // HEAD_DIM
    nblk = s // TOK_BLOCK
    blk = (TOK_BLOCK, HEAD_DIM)
    hblk = (None,) + blk
    rev = lambda h, i: (h, nblk - 1 - i, 0)

    def col(group):
        return pl.BlockSpec(hblk, lambda h, i, g=group: (g * nh + h, nblk - 1 - i, 0))

    return pl.pallas_call(
        _rec_bwd_kernel,
        grid=(nh, nblk),
        in_specs=[
            pl.BlockSpec(hblk, rev),
            pl.BlockSpec(hblk, rev),
            pl.BlockSpec((None, FACB_ROWS, HEAD_DIM), rev),
            pl.BlockSpec((None, CHUNKS_PER_BLOCK, HEAD_DIM), rev),
            col(3),
            pl.BlockSpec(hblk, rev),
            col(4),
            pl.BlockSpec((1, HEAD_DIM), lambda h, i: (0, h)),
        ],
        out_specs=pl.BlockSpec(blk, lambda h, i: (nblk - 1 - i, h)),
        out_shape=jax.ShapeDtypeStruct((s, d_hgrn), BF16),
        scratch_shapes=[pltpu.VMEM(blk, F32), pltpu.VMEM((HEAD_DIM, HEAD_DIM), F32)],
        compiler_params=_params(("parallel", "arbitrary")),
        name="rec_bwd",
    )(qsb, kpb, facb, tb, proj, o_part, proj, norm_w)


def _conv_kernel(b_ref, c_ref, h_ref, cp_ref, hp_ref, cn_ref, hn_ref, w_ref, y_ref):
    i = pl.program_id(0)
    last = pl.num_programs(0) - 1
    n = c_ref.shape[1]
    sub = lax.broadcasted_iota(jnp.int32, (SUBLANES, LANES), 0)
    for c in range(c_ref.shape[0]):
        u = c_ref[c] * h_ref[c]
        halo_p = jnp.where(i > 0, cp_ref[c] * hp_ref[c], 0.0)
        first = jnp.where(sub == 0, pltpu.roll(halo_p, 1, 0), pltpu.roll(u[n - SUBLANES:], 1, 0))
        u_prev = jnp.concatenate([first, u[:n - SUBLANES]], axis=0)
        halo_n = jnp.where(i < last, cn_ref[c] * hn_ref[c], 0.0)
        lastv = jnp.where(sub == SUBLANES - 1, pltpu.roll(halo_n, SUBLANES - 1, 0),
                          pltpu.roll(u[:SUBLANES], SUBLANES - 1, 0))
        u_next = jnp.concatenate([u[SUBLANES:], lastv], axis=0)
        w = w_ref[:, c * LANES:(c + 1) * LANES]
        y = w[0:1] * u_prev + w[1:2] * u + w[2:3] * u_next
        y_ref[:, c * LANES:(c + 1) * LANES] = (b_ref[c] * y).astype(y_ref.dtype)


def _conv(proj, conv_w, d_hgrn, d_conv, *, tc):
    s = proj.shape[1]
    nblk = s // TOK_BLOCK
    nrow8 = s // SUBLANES
    per_blk = TOK_BLOCK // SUBLANES
    slabs = tc // LANES
    base = 5 * d_hgrn // tc
    ncol = d_conv // tc

    def main(group):
        return pl.BlockSpec((slabs, TOK_BLOCK, LANES),
                            lambda i, j, g=group: (base + g * ncol + j, i, 0))

    def prev(group):
        return pl.BlockSpec((slabs, SUBLANES, LANES),
                            lambda i, j, g=group: (base + g * ncol + j,
                                                   jnp.maximum(i * per_blk - 1, 0), 0))

    def nxt(group):
        return pl.BlockSpec((slabs, SUBLANES, LANES),
                            lambda i, j, g=group: (base + g * ncol + j,
                                                   jnp.minimum((i + 1) * per_blk, nrow8 - 1), 0))

    return pl.pallas_call(
        _conv_kernel,
        grid=(nblk, ncol),
        in_specs=[main(0), main(1), main(2), prev(1), prev(2), nxt(1), nxt(2),
                  pl.BlockSpec((CONV_WIDTH, tc), lambda i, j: (0, j))],
        out_specs=pl.BlockSpec((TOK_BLOCK, tc), lambda i, j: (i, j)),
        out_shape=jax.ShapeDtypeStruct((s, d_conv), BF16),
        compiler_params=_params(("parallel", "parallel")),
        name="short_conv",
    )(proj, proj, proj, proj, proj, proj, proj, conv_w)


def _out_proj_kernel(x_ref, yr_ref, yc_ref, wr_ref, wc_ref, o_ref):
    acc = jnp.dot(yr_ref[...], wr_ref[...], preferred_element_type=F32)
    acc = acc + jnp.dot(yc_ref[...], wc_ref[...], preferred_element_type=F32)
    o_ref[...] = x_ref[...] + acc


def _out_proj(x, y_rec, y_conv, w, layer, *, tm):
    s, d = x.shape
    d_rec = y_rec.shape[1]
    d_conv = y_conv.shape[1]
    assert d_rec == d_conv
    return pl.pallas_call(
        _out_proj_kernel,
        grid=(s // tm,),
        in_specs=[
            pl.BlockSpec((tm, d), lambda i: (i, 0)),
            pl.BlockSpec((tm, d_rec), lambda i: (i, 0)),
            pl.BlockSpec((tm, d_conv), lambda i: (i, 0)),
            pl.BlockSpec((None, d_rec, d), lambda i: (layer, 0, 0)),
            pl.BlockSpec((None, d_conv, d), lambda i: (layer, 1, 0)),
        ],
        out_specs=pl.BlockSpec((tm, d), lambda i: (i, 0)),
        out_shape=jax.ShapeDtypeStruct((s, d), F32),
        compiler_params=_params(("parallel",)),
        name="out_proj",
    )(x, y_rec, y_conv, w, w)


def _ffn_kernel(final, x_ref, nw_ref, wg_ref, wu_ref, wd_ref, fw_ref, o_ref, h_scr, acc_scr):
    f = pl.program_id(1)

    @pl.when(f == 0)
    def _():
        h_scr[...] = _rms(x_ref[...], nw_ref[...]).astype(BF16)
        acc_scr[...] = jnp.zeros_like(acc_scr)

    h = h_scr[...]
    g = jnp.dot(h, wg_ref[...], preferred_element_type=F32)
    u = jnp.dot(h, wu_ref[...], preferred_element_type=F32)
    a = (g * _sigmoid(g) * u).astype(BF16)
    acc_scr[...] += jnp.dot(a, wd_ref[...], preferred_element_type=F32)

    @pl.when(f == pl.num_programs(1) - 1)
    def _():
        y = x_ref[...] + acc_scr[...]
        if final:
            y = _rms(y, fw_ref[...])
        o_ref[...] = y


def _ffn(x, nw, w_gu, w_d, fw, layer, final, *, tm, tf):
    s, d = x.shape
    d_ff = w_d.shape[1]
    nf = d_ff // tf
    return pl.pallas_call(
        functools.partial(_ffn_kernel, final),
        grid=(s // tm, nf),
        in_specs=[
            pl.BlockSpec((tm, d), lambda i, f: (i, 0)),
            pl.BlockSpec((1, d), lambda i, f: (0, 0)),
            pl.BlockSpec((None, d, tf), lambda i, f: (layer, 0, f)),
            pl.BlockSpec((None, d, tf), lambda i, f: (layer, 0, nf + f)),
            pl.BlockSpec((None, tf, d), lambda i, f: (layer, f, 0)),
            pl.BlockSpec((1, d), lambda i, f: (0, 0)),
        ],
        out_specs=pl.BlockSpec((tm, d), lambda i, f: (i, 0)),
        out_shape=jax.ShapeDtypeStruct((s, d), F32),
        scratch_shapes=[pltpu.VMEM((tm, d), BF16), pltpu.VMEM((tm, d), F32)],
        compiler_params=_params(("parallel", "arbitrary")),
        name="ffn",
    )(x, nw, w_gu, w_gu, w_d, fw)


def _interleave(x):
    s, d = x.shape
    x4 = x.reshape(s // TOK_BLOCK, CHUNKS_PER_BLOCK, CHUNK, d)
    return x4.transpose(0, 2, 1, 3).reshape(s, d)


def _deinterleave(x):
    s, d = x.shape
    x4 = x.reshape(s // TOK_BLOCK, CHUNK, CHUNKS_PER_BLOCK, d)
    return x4.transpose(0, 2, 1, 3).reshape(s, d)


def kernel(x, attn_norm_w, w_in, lb_fwd, lb_bwd, hgrn_norm_w, conv_w, w_out, ffn_norm_w,
           w_gate_up, w_down, final_norm_w):
    bsz, s, d = x.shape
    depth = w_in.shape[0]
    d_hgrn = lb_fwd.shape[1]
    d_conv = conv_w.shape[2]
    assert bsz == 1 and s % TOK_BLOCK == 0
    assert w_in.shape[2] == 5 * d_hgrn + 3 * d_conv and d_hgrn == d_conv

    w_in, w_out, w_gate_up, w_down = (_cast_bf16(w) for w in (w_in, w_out, w_gate_up, w_down))
    xs = _interleave(x[0])
    fw = final_norm_w.reshape(1, d)
    for l in range(depth):
        proj = _in_proj(xs, attn_norm_w[l].reshape(1, d), w_in, l, tm=1024, tn=1024)
        o_part, qsb, kpb, facb, tb = _rec_fwd(proj, lb_fwd, lb_bwd, l, d_hgrn)
        y_rec = _rec_bwd(qsb, kpb, facb, tb, proj, o_part, hgrn_norm_w[l].reshape(1, d_hgrn), d_hgrn)
        y_conv = _conv(proj, conv_w[l], d_hgrn, d_conv, tc=512)
        xs = _out_proj(xs, y_rec, y_conv, w_out, l, tm=512)
        xs = _ffn(xs, ffn_norm_w[l].reshape(1, d), w_gate_up, w_down, fw, l, l == depth - 1,
                  tm=512, tf=512)
    return _deinterleave(xs)[None]
```

```python
import functools

import jax
import jax.numpy as jnp
from jax import lax
from jax.experimental import pallas as pl
from jax.experimental.pallas import tpu as pltpu

F32 = jnp.float32
BF16 = jnp.bfloat16

EPS = 1e-6
HEAD_DIM = 128
CHUNK = 128
SUBLANES = 8
LANES = 128
CHUNKS_PER_BLOCK = SUBLANES
TOK_BLOCK = CHUNK * CHUNKS_PER_BLOCK
POS_BLOCK = SUBLANES
N_GROUPS = CHUNK // POS_BLOCK
GROUP_ROWS = POS_BLOCK * CHUNKS_PER_BLOCK
LOW_LEVELS = tuple(1 << b for b in range(POS_BLOCK.bit_length() - 1))
HIGH_SPANS = tuple(1 << b for b in range(N_GROUPS.bit_length() - 1))
N_LOW = len(LOW_LEVELS) + 1
N_LEVELS = N_LOW + len(HIGH_SPANS)
N_FAC_LEVEL = 2 * (len(HIGH_SPANS) - 1)
FAC_QF, FAC_KF, FAC_QB, FAC_KB = (N_FAC_LEVEL + i for i in range(4))
FAC_RQF, FAC_RKF, FAC_RQB, FAC_RKB = (N_FAC_LEVEL + 4 + i for i in range(4))
N_FAC = N_FAC_LEVEL + 8
FAC_ROWS = N_FAC * N_GROUPS * CHUNKS_PER_BLOCK
BASE_QP, BASE_KS, BASE_QS, BASE_KP, BASE_QIS, BASE_KIP, BASE_QIP, BASE_KIS = range(8)
N_BASE = 8
RATIO_SPAN = 4
RATIO_MIN = 2.0 ** -100
BWD_BLOCKS = 2
CONV_WIDTH = 3
VMEM_LIMIT = 52 * 1024 * 1024
CAST_BLOCK_BYTES = 8 * 1024 * 1024

_NT = (((1,), (1,)), ((), ()))
_TN = (((0,), (0,)), ((), ()))


def _sigmoid(x):
    return 1.0 / (1.0 + jnp.exp(-x))


def _rms(x, w):
    ms = jnp.mean(x * x, axis=-1, keepdims=True)
    return x * lax.rsqrt(ms + EPS) * w


def _params(sem):
    return pltpu.CompilerParams(dimension_semantics=sem, vmem_limit_bytes=VMEM_LIMIT)


def _cast_kernel(w_ref, o_ref):
    o_ref[...] = w_ref[...].astype(o_ref.dtype)


def _cast_bf16(w):
    depth, r, c = w.shape
    rows = depth * r
    tr = 1 << ((CAST_BLOCK_BYTES // (4 * c)).bit_length() - 1)
    assert tr >= 2 * SUBLANES and rows % tr == 0
    out = pl.pallas_call(
        _cast_kernel,
        grid=(rows // tr,),
        in_specs=[pl.BlockSpec((tr, c), lambda i: (i, 0))],
        out_specs=pl.BlockSpec((tr, c), lambda i: (i, 0)),
        out_shape=jax.ShapeDtypeStruct((rows, c), BF16),
        compiler_params=_params(("parallel",)),
        name="cast_bf16",
    )(w.reshape(rows, c))
    return out.reshape(depth, r, c)


def _in_proj_kernel(x_ref, nw_ref, w_ref, o_ref, h_scr):
    @pl.when(pl.program_id(1) == 0)
    def _():
        h_scr[...] = _rms(x_ref[...], nw_ref[...]).astype(BF16)

    res = jnp.dot(h_scr[...], w_ref[...], preferred_element_type=F32)
    for c in range(o_ref.shape[0]):
        o_ref[c] = res[:, c * LANES:(c + 1) * LANES]


def _in_proj(x, nw, w, layer, *, tm, tn):
    s, d = x.shape
    n = w.shape[2]
    return pl.pallas_call(
        _in_proj_kernel,
        grid=(s // tm, n // tn),
        in_specs=[
            pl.BlockSpec((tm, d), lambda i, j: (i, 0)),
            pl.BlockSpec((1, d), lambda i, j: (0, 0)),
            pl.BlockSpec((None, d, tn), lambda i, j: (layer, 0, j)),
        ],
        out_specs=pl.BlockSpec((tn // LANES, tm, LANES), lambda i, j: (j, i, 0)),
        out_shape=jax.ShapeDtypeStruct((n // LANES, s, LANES), F32),
        scratch_shapes=[pltpu.VMEM((tm, d), BF16)],
        compiler_params=_params(("parallel", "arbitrary")),
        name="in_proj",
    )(x, nw, w)


def _lower_bound(lb_ref, layer):
    p = lb_ref[...]
    if layer == 0:
        return jnp.zeros((1, p.shape[1]), F32)
    e = jnp.exp(p - jnp.max(p, axis=0, keepdims=True))
    sm = e / jnp.sum(e, axis=0, keepdims=True)
    return jnp.sum(sm[1:layer + 1], axis=0, keepdims=True)


def _gates(z, lb):
    t = jnp.exp(-jnp.abs(z))
    r = 1.0 / (1.0 + t)
    tr = t * r
    pos = z >= 0
    f = lb + (1.0 - lb) * jnp.where(pos, r, tr)
    k = (1.0 - lb) * jnp.where(pos, tr, r)
    return f, k


def _halves(x, h):
    x5 = x.reshape((x.shape[0] // (2 * h), 2, h) + x.shape[1:])
    return x5[:, 0], x5[:, 1]


def _join(a, b):
    x = jnp.stack([a, b], axis=1)
    return x.reshape((x.shape[0] * x.shape[1] * x.shape[2],) + a.shape[2:])


def _chunk_rows(ref, r):
    return ref[pl.ds(r, CHUNK, stride=CHUNKS_PER_BLOCK), :]


def _fac_rows(k, g):
    return pl.ds((k * N_GROUPS + g) * CHUNKS_PER_BLOCK, CHUNKS_PER_BLOCK)


def _group_factor(fac_ref, k, r):
    return jnp.concatenate(
        [jnp.broadcast_to(fac_ref[pl.ds((k * N_GROUPS + g) * CHUNKS_PER_BLOCK + r, 1), :],
                          (POS_BLOCK, HEAD_DIM))
         for g in range(N_GROUPS)], axis=0)


def _by_group_parity(even, odd, span):
    pieces = []
    for g in range(N_GROUPS):
        src = odd if (g // span) % 2 else even
        pieces.append(src[g * POS_BLOCK:(g + 1) * POS_BLOCK])
    return jnp.concatenate(pieces, axis=0)


def _seg_products(tot, span):
    ones = jnp.ones_like(tot[0])
    pre, suf = [None] * N_GROUPS, [None] * N_GROUPS
    for s0 in range(0, N_GROUPS, span):
        acc = None
        for g in range(s0, s0 + span):
            pre[g] = ones if acc is None else acc
            acc = tot[g] if acc is None else acc * tot[g]
        acc = None
        for g in reversed(range(s0, s0 + span)):
            suf[g] = ones if acc is None else acc
            acc = tot[g] if acc is None else acc * tot[g]
    return pre, suf


def _group_scan(g, q_ref, zf_ref, zb_ref, lbf, lbb, low_stage=None):
    sh = (POS_BLOCK, CHUNKS_PER_BLOCK, HEAD_DIM)
    flat = (GROUP_ROWS, HEAD_DIM)
    rows = pl.ds(g * GROUP_ROWS, GROUP_ROWS)
    q = q_ref[rows, :].reshape(sh)
    qs = q * _sigmoid(q)
    f_f, k_f = _gates(zf_ref[rows, :].reshape(sh), lbf)
    f_b, k_b = _gates(zb_ref[rows, :].reshape(sh), lbb)
    if low_stage is not None:
        stq_ref, stk_ref = low_stage
        stq_ref[0, rows, :] = qs.reshape(flat)
        stk_ref[0, rows, :] = (k_f + k_b).reshape(flat)
    pin, sex, tf = f_f, jnp.ones(sh, F32), f_f
    sin, pex, tb = f_b, jnp.ones(sh, F32), f_b
    for li, h in enumerate(LOW_LEVELS):
        pin1, pin2 = _halves(pin, h)
        sex1, sex2 = _halves(sex, h)
        sin1, sin2 = _halves(sin, h)
        pex1, pex2 = _halves(pex, h)
        if low_stage is not None:
            qs1, qs2 = _halves(qs, h)
            kf1, _ = _halves(k_f, h)
            _, kb2 = _halves(k_b, h)
            stq_ref[li + 1, rows, :] = _join(qs1 * sin1, qs2 * pin2).reshape(flat)
            stk_ref[li + 1, rows, :] = _join(kf1 * sex1, kb2 * pex2).reshape(flat)
        tf1, tf2 = _halves(tf, 1)
        tb1, tb2 = _halves(tb, 1)
        pin = _join(pin1, pin2 * tf1)
        sex = _join(sex1 * tf2, sex2)
        sin = _join(sin1 * tb2, sin2)
        pex = _join(pex1, pex2 * tb1)
        tf = (tf1 * tf2).reshape((tf.shape[0] // 2,) + sh[1:])
        tb = (tb1 * tb2).reshape((tb.shape[0] // 2,) + sh[1:])
    return rows, qs, k_f, k_b, pin, sex, sin, pex, tf[0], tb[0]


def _chunk_factors(tot_f, tot_b, fac_ref, tf_ref, tb_ref):
    pre_f, suf_f = _seg_products(tot_f, N_GROUPS)
    pre_b, suf_b = _seg_products(tot_b, N_GROUPS)
    for g in range(N_GROUPS):
        fac_ref[_fac_rows(FAC_QF, g), :] = pre_f[g]
        fac_ref[_fac_rows(FAC_KF, g), :] = suf_f[g]
        fac_ref[_fac_rows(FAC_QB, g), :] = suf_b[g]
        fac_ref[_fac_rows(FAC_KB, g), :] = pre_b[g]
    tf_ref[...] = pre_f[N_GROUPS - 1] * tot_f[N_GROUPS - 1]
    tb_ref[...] = pre_b[N_GROUPS - 1] * tot_b[N_GROUPS - 1]


def _level_factors(tot_f, tot_b, fac_ref, si):
    span = HIGH_SPANS[si]
    pre_f, suf_f = _seg_products(tot_f, span)
    pre_b, suf_b = _seg_products(tot_b, span)
    for g in range(N_GROUPS):
        second = (g // span) % 2
        fac_ref[_fac_rows(2 * (si - 1), g), :] = pre_f[g] if second else suf_b[g]
        fac_ref[_fac_rows(2 * (si - 1) + 1, g), :] = pre_b[g] if second else suf_f[g]


def _scan_ratio(layer, lbf_ref, lbb_ref, q_ref, zf_ref, zb_ref, tb_ref, stage):
    _, _, base_ref, fac_ref, tf_ref = stage
    lbf = _lower_bound(lbf_ref, layer)
    lbb = _lower_bound(lbb_ref, layer)
    flat = (GROUP_ROWS, HEAD_DIM)
    tot_f, tot_b = [], []
    for g in range(N_GROUPS):
        rows, qs, k_f, k_b, pin, sex, sin, pex, tf, tb = _group_scan(
            g, q_ref, zf_ref, zb_ref, lbf, lbb)
        base_ref[BASE_QP, rows, :] = (qs * pin).reshape(flat)
        base_ref[BASE_KS, rows, :] = (k_f * sex).reshape(flat)
        base_ref[BASE_QS, rows, :] = (qs * sin).reshape(flat)
        base_ref[BASE_KP, rows, :] = (k_b * pex).reshape(flat)
        base_ref[BASE_QIS, rows, :] = (qs * (1.0 / sex)).reshape(flat)
        base_ref[BASE_KIP, rows, :] = (k_f * (1.0 / pin)).reshape(flat)
        base_ref[BASE_QIP, rows, :] = (qs * (1.0 / pex)).reshape(flat)
        base_ref[BASE_KIS, rows, :] = (k_b * (1.0 / sin)).reshape(flat)
        tot_f.append(tf)
        tot_b.append(tb)

    pre_f, suf_f = _seg_products(tot_f, RATIO_SPAN)
    pre_b, suf_b = _seg_products(tot_b, RATIO_SPAN)
    smallest = None
    for g in range(N_GROUPS):
        second = (g // RATIO_SPAN) % 2
        fac_ref[_fac_rows(FAC_RQF, g), :] = pre_f[g] if second else 1.0 / suf_f[g]
        fac_ref[_fac_rows(FAC_RKF, g), :] = 1.0 / pre_f[g] if second else suf_f[g]
        fac_ref[_fac_rows(FAC_RQB, g), :] = 1.0 / pre_b[g] if second else suf_b[g]
        fac_ref[_fac_rows(FAC_RKB, g), :] = pre_b[g] if second else 1.0 / suf_b[g]
        if g % RATIO_SPAN == RATIO_SPAN - 1:
            run = jnp.minimum(pre_f[g] * tot_f[g], pre_b[g] * tot_b[g])
            smallest = run if smallest is None else jnp.minimum(smallest, run)
    _level_factors(tot_f, tot_b, fac_ref, len(HIGH_SPANS) - 1)
    _chunk_factors(tot_f, tot_b, fac_ref, tf_ref, tb_ref)
    return jnp.min(smallest)


def _scan_levels(layer, lbf_ref, lbb_ref, q_ref, zf_ref, zb_ref, tb_ref, stage):
    stq_ref, stk_ref, base_ref, fac_ref, tf_ref = stage
    lbf = _lower_bound(lbf_ref, layer)
    lbb = _lower_bound(lbb_ref, layer)
    flat = (GROUP_ROWS, HEAD_DIM)
    tot_f, tot_b = [], []
    for g in range(N_GROUPS):
        rows, qs, k_f, k_b, pin, sex, sin, pex, tf, tb = _group_scan(
            g, q_ref, zf_ref, zb_ref, lbf, lbb, (stq_ref, stk_ref))
        base_ref[BASE_QP, rows, :] = (qs * pin).reshape(flat)
        base_ref[BASE_KS, rows, :] = (k_f * sex).reshape(flat)
        base_ref[BASE_QS, rows, :] = (qs * sin).reshape(flat)
        base_ref[BASE_KP, rows, :] = (k_b * pex).reshape(flat)
        tot_f.append(tf)
        tot_b.append(tb)
    for si in range(1, len(HIGH_SPANS)):
        _level_factors(tot_f, tot_b, fac_ref, si)
    _chunk_factors(tot_f, tot_b, fac_ref, tf_ref, tb_ref)


def _chunk_bases(base_ref, r):
    return tuple(_chunk_rows(base_ref.at[k], r) for k in (BASE_QP, BASE_KS, BASE_QS, BASE_KP))


def _high_level(bases, fac_ref, si, r):
    qp, ks, qs, kp = bases
    span = HIGH_SPANS[si]
    ql = _by_group_parity(qs, qp, span)
    kl = _by_group_parity(ks, kp, span)
    if si > 0:
        ql = ql * _group_factor(fac_ref, 2 * (si - 1), r)
        kl = kl * _group_factor(fac_ref, 2 * (si - 1) + 1, r)
    return ql.astype(BF16), kl.astype(BF16)


def _nt_dot(a, b):
    return lax.dot_general(a, b, _NT, preferred_element_type=F32)


def _finish_chunk(r, a, bases, v_ref, o_ref, qb_ref, kb_ref, st_ref, fac_ref, tf_ref):
    qp, ks, qs, kp = bases
    vb = _chunk_rows(v_ref, r).astype(BF16)
    st = st_ref[...]
    o = jnp.dot(a.astype(BF16), vb, preferred_element_type=F32)
    qf = (qp * _group_factor(fac_ref, FAC_QF, r)).astype(BF16)
    o = o + _nt_dot(qf, st.astype(BF16))
    kf = (ks * _group_factor(fac_ref, FAC_KF, r)).astype(BF16)
    upd = lax.dot_general(vb, kf, _TN, preferred_element_type=F32)
    st_ref[...] = st * tf_ref[pl.ds(r, 1), :] + upd
    o_ref[pl.ds(r, CHUNK, stride=CHUNKS_PER_BLOCK), :] = o
    rows = pl.ds(r * CHUNK, CHUNK)
    qb_ref[rows, :] = (qs * _group_factor(fac_ref, FAC_QB, r)).astype(BF16)
    kb_ref[rows, :] = (kp * _group_factor(fac_ref, FAC_KB, r)).astype(BF16)


def _matmul_ratio(v_ref, o_ref, qb_ref, kb_ref, st_ref, stage):
    _, _, base_ref, fac_ref, tf_ref = stage
    row = lax.broadcasted_iota(jnp.int32, (CHUNK, CHUNK), 0)
    col = lax.broadcasted_iota(jnp.int32, (CHUNK, CHUNK), 1)
    far = (row ^ col) >= 2 * RATIO_SPAN * POS_BLOCK
    top = len(HIGH_SPANS) - 1
    for r in range(CHUNKS_PER_BLOCK):
        bases = _chunk_bases(base_ref, r)
        qp, ks, qs, kp = bases
        qis, kip, qip, kis = (_chunk_rows(base_ref.at[k], r)
                              for k in (BASE_QIS, BASE_KIP, BASE_QIP, BASE_KIS))
        qf = _by_group_parity(qis, qp, RATIO_SPAN) * _group_factor(fac_ref, FAC_RQF, r)
        kf = _by_group_parity(ks, kip, RATIO_SPAN) * _group_factor(fac_ref, FAC_RKF, r)
        qb = _by_group_parity(qs, qip, RATIO_SPAN) * _group_factor(fac_ref, FAC_RQB, r)
        kb = _by_group_parity(kis, kp, RATIO_SPAN) * _group_factor(fac_ref, FAC_RKB, r)
        p_f = _nt_dot(qf.astype(BF16), kf.astype(BF16))
        p_b = _nt_dot(qb.astype(BF16), kb.astype(BF16))
        p_top = _nt_dot(*_high_level(bases, fac_ref, top, r))
        a = jnp.where(far, p_top, jnp.where(row >= col, p_f, p_b))
        a = a + jnp.where(row == col, p_b, 0.0)
        _finish_chunk(r, a, bases, v_ref, o_ref, qb_ref, kb_ref, st_ref, fac_ref, tf_ref)


def _matmul_levels(v_ref, o_ref, qb_ref, kb_ref, st_ref, stage):
    stq_ref, stk_ref, base_ref, fac_ref, tf_ref = stage
    row = lax.broadcasted_iota(jnp.int32, (CHUNK, CHUNK), 0)
    col = lax.broadcasted_iota(jnp.int32, (CHUNK, CHUNK), 1)
    xr = row ^ col
    lvl = jnp.zeros((CHUNK, CHUNK), jnp.int32)
    for b in range(N_LEVELS - 1):
        lvl = lvl + jnp.where(xr >= (1 << b), 1, 0)

    for r in range(CHUNKS_PER_BLOCK):
        bases = _chunk_bases(base_ref, r)
        a = None
        for li in range(N_LEVELS):
            if li < N_LOW:
                ql = _chunk_rows(stq_ref.at[li], r).astype(BF16)
                kl = _chunk_rows(stk_ref.at[li], r).astype(BF16)
            else:
                ql, kl = _high_level(bases, fac_ref, li - N_LOW, r)
            p = _nt_dot(ql, kl)
            a = p if a is None else jnp.where(lvl == li, p, a)
        _finish_chunk(r, a, bases, v_ref, o_ref, qb_ref, kb_ref, st_ref, fac_ref, tf_ref)


def _rec_fwd_kernel(layer, lbf_ref, lbb_ref, q_ref, zf_ref, zb_ref, v_ref,
                    o_ref, qb_ref, kb_ref, tb_ref, *scratch):
    stage, st_ref = scratch[:-1], scratch[-1]

    @pl.when(pl.program_id(1) == 0)
    def _():
        st_ref[...] = jnp.zeros_like(st_ref)

    smallest = _scan_ratio(layer, lbf_ref, lbb_ref, q_ref, zf_ref, zb_ref, tb_ref, stage)
    safe = smallest >= RATIO_MIN

    @pl.when(safe)
    def _():
        _matmul_ratio(v_ref, o_ref, qb_ref, kb_ref, st_ref, stage)

    @pl.when(jnp.logical_not(safe))
    def _():
        _scan_levels(layer, lbf_ref, lbb_ref, q_ref, zf_ref, zb_ref, tb_ref, stage)
        _matmul_levels(v_ref, o_ref, qb_ref, kb_ref, st_ref, stage)


def _rec_fwd(proj, lb_fwd, lb_bwd, layer, d_hgrn):
    s = proj.shape[1]
    depth = lb_fwd.shape[0]
    nh = d_hgrn // HEAD_DIM
    nblk = s // TOK_BLOCK
    blk = (TOK_BLOCK, HEAD_DIM)
    hblk = (None,) + blk
    here = lambda h, i: (h, i, 0)

    def col(group):
        return pl.BlockSpec(hblk, lambda h, i, g=group: (g * nh + h, i, 0))

    lb_spec = pl.BlockSpec((depth, HEAD_DIM), lambda h, i: (0, h))
    return pl.pallas_call(
        functools.partial(_rec_fwd_kernel, layer),
        grid=(nh, nblk),
        in_specs=[lb_spec, lb_spec, col(0), col(1), col(2), col(3)],
        out_specs=[
            pl.BlockSpec(hblk, here),
            pl.BlockSpec(hblk, here),
            pl.BlockSpec(hblk, here),
            pl.BlockSpec((None, CHUNKS_PER_BLOCK, HEAD_DIM), here),
        ],
        out_shape=[jax.ShapeDtypeStruct((nh, s, HEAD_DIM), F32),
                   jax.ShapeDtypeStruct((nh, s, HEAD_DIM), BF16),
                   jax.ShapeDtypeStruct((nh, s, HEAD_DIM), BF16),
                   jax.ShapeDtypeStruct((nh, nblk * CHUNKS_PER_BLOCK, HEAD_DIM), F32)],
        scratch_shapes=[
            pltpu.VMEM((N_LOW,) + blk, F32),
            pltpu.VMEM((N_LOW,) + blk, F32),
            pltpu.VMEM((N_BASE,) + blk, F32),
            pltpu.VMEM((FAC_ROWS, HEAD_DIM), F32),
            pltpu.VMEM((CHUNKS_PER_BLOCK, HEAD_DIM), F32),
            pltpu.VMEM((HEAD_DIM, HEAD_DIM), F32),
        ],
        compiler_params=_params(("parallel", "arbitrary")),
        name="rec_fwd",
    )(lb_fwd, lb_bwd, proj, proj, proj, proj)


def _rec_bwd_kernel(qb_ref, kb_ref, tb_ref, v_ref, op_ref, g_ref, nw_ref, y_ref, ob_ref, st_ref):
    @pl.when(pl.program_id(1) == 0)
    def _():
        st_ref[...] = jnp.zeros_like(st_ref)

    for blk in reversed(range(BWD_BLOCKS)):
        for r in reversed(range(CHUNKS_PER_BLOCK)):
            st = st_ref[...]
            rows = pl.ds(blk * TOK_BLOCK + r * CHUNK, CHUNK)
            strided = pl.ds(blk * TOK_BLOCK + r, CHUNK, stride=CHUNKS_PER_BLOCK)
            ob_ref[strided, :] = _nt_dot(qb_ref[rows, :], st.astype(BF16))
            upd = lax.dot_general(v_ref[strided, :].astype(BF16), kb_ref[rows, :], _TN,
                                  preferred_element_type=F32)
            st_ref[...] = st * tb_ref[pl.ds(blk * CHUNKS_PER_BLOCK + r, 1), :] + upd
    g = g_ref[...]
    y = _rms(op_ref[...] + ob_ref[...], nw_ref[...]) * (g * _sigmoid(g))
    y_ref[...] = y.astype(y_ref.dtype)


def _rec_bwd(qb, kb, tb, proj, o_part, norm_w, d_hgrn):
    s = proj.shape[1]
    nh = d_hgrn // HEAD_DIM
    rows = BWD_BLOCKS * TOK_BLOCK
    nstep = s // rows
    blk = (rows, HEAD_DIM)
    hblk = (None,) + blk
    rev = lambda h, i: (h, nstep - 1 - i, 0)

    def col(group):
        return pl.BlockSpec(hblk, lambda h, i, g=group: (g * nh + h, nstep - 1 - i, 0))

    return pl.pallas_call(
        _rec_bwd_kernel,
        grid=(nh, nstep),
        in_specs=[
            pl.BlockSpec(hblk, rev),
            pl.BlockSpec(hblk, rev),
            pl.BlockSpec((None, BWD_BLOCKS * CHUNKS_PER_BLOCK, HEAD_DIM), rev),
            col(3),
            pl.BlockSpec(hblk, rev),
            col(4),
            pl.BlockSpec((1, HEAD_DIM), lambda h, i: (0, h)),
        ],
        out_specs=pl.BlockSpec(blk, lambda h, i: (nstep - 1 - i, h)),
        out_shape=jax.ShapeDtypeStruct((s, d_hgrn), BF16),
        scratch_shapes=[pltpu.VMEM(blk, F32), pltpu.VMEM((HEAD_DIM, HEAD_DIM), F32)],
        compiler_params=_params(("parallel", "arbitrary")),
        name="rec_bwd",
    )(qb, kb, tb, proj, o_part, proj, norm_w)


def _conv_kernel(b_ref, c_ref, h_ref, cp_ref, hp_ref, cn_ref, hn_ref, w_ref, y_ref):
    i = pl.program_id(0)
    last = pl.num_programs(0) - 1
    n = c_ref.shape[1]
    sub = lax.broadcasted_iota(jnp.int32, (SUBLANES, LANES), 0)
    for c in range(c_ref.shape[0]):
        u = c_ref[c] * h_ref[c]
        halo_p = jnp.where(i > 0, cp_ref[c] * hp_ref[c], 0.0)
        first = jnp.where(sub == 0, pltpu.roll(halo_p, 1, 0), pltpu.roll(u[n - SUBLANES:], 1, 0))
        u_prev = jnp.concatenate([first, u[:n - SUBLANES]], axis=0)
        halo_n = jnp.where(i < last, cn_ref[c] * hn_ref[c], 0.0)
        lastv = jnp.where(sub == SUBLANES - 1, pltpu.roll(halo_n, SUBLANES - 1, 0),
                          pltpu.roll(u[:SUBLANES], SUBLANES - 1, 0))
        u_next = jnp.concatenate([u[SUBLANES:], lastv], axis=0)
        w = w_ref[:, c * LANES:(c + 1) * LANES]
        y = w[0:1] * u_prev + w[1:2] * u + w[2:3] * u_next
        y_ref[:, c * LANES:(c + 1) * LANES] = (b_ref[c] * y).astype(y_ref.dtype)


def _conv(proj, conv_w, d_hgrn, d_conv, *, tc):
    s = proj.shape[1]
    nblk = s // TOK_BLOCK
    nrow8 = s // SUBLANES
    per_blk = TOK_BLOCK // SUBLANES
    slabs = tc // LANES
    base = 5 * d_hgrn // tc
    ncol = d_conv // tc

    def main(group):
        return pl.BlockSpec((slabs, TOK_BLOCK, LANES),
                            lambda i, j, g=group: (base + g * ncol + j, i, 0))

    def prev(group):
        return pl.BlockSpec((slabs, SUBLANES, LANES),
                            lambda i, j, g=group: (base + g * ncol + j,
                                                   jnp.maximum(i * per_blk - 1, 0), 0))

    def nxt(group):
        return pl.BlockSpec((slabs, SUBLANES, LANES),
                            lambda i, j, g=group: (base + g * ncol + j,
                                                   jnp.minimum((i + 1) * per_blk, nrow8 - 1), 0))

    return pl.pallas_call(
        _conv_kernel,
        grid=(nblk, ncol),
        in_specs=[main(0), main(1), main(2), prev(1), prev(2), nxt(1), nxt(2),
                  pl.BlockSpec((CONV_WIDTH, tc), lambda i, j: (0, j))],
        out_specs=pl.BlockSpec((TOK_BLOCK, tc), lambda i, j: (i, j)),
        out_shape=jax.ShapeDtypeStruct((s, d_conv), BF16),
        compiler_params=_params(("parallel", "parallel")),
        name="short_conv",
    )(proj, proj, proj, proj, proj, proj, proj, conv_w)


def _out_proj_kernel(x_ref, yr_ref, yc_ref, wr_ref, wc_ref, o_ref):
    acc = jnp.dot(yr_ref[...], wr_ref[...], preferred_element_type=F32)
    acc = acc + jnp.dot(yc_ref[...], wc_ref[...], preferred_element_type=F32)
    o_ref[...] = x_ref[...] + acc


def _out_proj(x, y_rec, y_conv, w, layer, *, tm):
    s, d = x.shape
    d_rec = y_rec.shape[1]
    d_conv = y_conv.shape[1]
    assert d_rec == d_conv
    return pl.pallas_call(
        _out_proj_kernel,
        grid=(s // tm,),
        in_specs=[
            pl.BlockSpec((tm, d), lambda i: (i, 0)),
            pl.BlockSpec((tm, d_rec), lambda i: (i, 0)),
            pl.BlockSpec((tm, d_conv), lambda i: (i, 0)),
            pl.BlockSpec((None, d_rec, d), lambda i: (layer, 0, 0)),
            pl.BlockSpec((None, d_conv, d), lambda i: (layer, 1, 0)),
        ],
        out_specs=pl.BlockSpec((tm, d), lambda i: (i, 0)),
        out_shape=jax.ShapeDtypeStruct((s, d), F32),
        compiler_params=_params(("parallel",)),
        name="out_proj",
    )(x, y_rec, y_conv, w, w)


def _ffn_kernel(final, x_ref, nw_ref, wg_ref, wu_ref, wd_ref, fw_ref, o_ref, h_scr, acc_scr):
    f = pl.program_id(1)

    @pl.when(f == 0)
    def _():
        h_scr[...] = _rms(x_ref[...], nw_ref[...]).astype(BF16)
        acc_scr[...] = jnp.zeros_like(acc_scr)

    h = h_scr[...]
    g = jnp.dot(h, wg_ref[...], preferred_element_type=F32)
    u = jnp.dot(h, wu_ref[...], preferred_element_type=F32)
    a = (g * _sigmoid(g) * u).astype(BF16)
    acc_scr[...] += jnp.dot(a, wd_ref[...], preferred_element_type=F32)

    @pl.when(f == pl.num_programs(1) - 1)
    def _():
        y = x_ref[...] + acc_scr[...]
        if final:
            y = _rms(y, fw_ref[...])
        o_ref[...] = y


def _ffn(x, nw, w_gu, w_d, fw, layer, final, *, tm, tf):
    s, d = x.shape
    d_ff = w_d.shape[1]
    nf = d_ff // tf
    return pl.pallas_call(
        functools.partial(_ffn_kernel, final),
        grid=(s // tm, nf),
        in_specs=[
            pl.BlockSpec((tm, d), lambda i, f: (i, 0)),
            pl.BlockSpec((1, d), lambda i, f: (0, 0)),
            pl.BlockSpec((None, d, tf), lambda i, f: (layer, 0, f)),
            pl.BlockSpec((None, d, tf), lambda i, f: (layer, 0, nf + f)),
            pl.BlockSpec((None, tf, d), lambda i, f: (layer, f, 0)),
            pl.BlockSpec((1, d), lambda i, f: (0, 0)),
        ],
        out_specs=pl.BlockSpec((tm, d), lambda i, f: (i, 0)),
        out_shape=jax.ShapeDtypeStruct((s, d), F32),
        scratch_shapes=[pltpu.VMEM((tm, d), BF16), pltpu.VMEM((tm, d), F32)],
        compiler_params=_params(("parallel", "arbitrary")),
        name="ffn",
    )(x, nw, w_gu, w_gu, w_d, fw)


def _interleave(x):
    s, d = x.shape
    x4 = x.reshape(s // TOK_BLOCK, CHUNKS_PER_BLOCK, CHUNK, d)
    return x4.transpose(0, 2, 1, 3).reshape(s, d)


def _deinterleave(x):
    s, d = x.shape
    x4 = x.reshape(s // TOK_BLOCK, CHUNK, CHUNKS_PER_BLOCK, d)
    return x4.transpose(0, 2, 1, 3).reshape(s, d)


def kernel(x, attn_norm_w, w_in, lb_fwd, lb_bwd, hgrn_norm_w, conv_w, w_out, ffn_norm_w,
           w_gate_up, w_down, final_norm_w):
    bsz, s, d = x.shape
    depth = w_in.shape[0]
    d_hgrn = lb_fwd.shape[1]
    d_conv = conv_w.shape[2]
    assert bsz == 1 and s % (BWD_BLOCKS * TOK_BLOCK) == 0
    assert w_in.shape[2] == 5 * d_hgrn + 3 * d_conv and d_hgrn == d_conv

    w_in, w_out, w_gate_up, w_down = (_cast_bf16(w) for w in (w_in, w_out, w_gate_up, w_down))
    xs = _interleave(x[0])
    fw = final_norm_w.reshape(1, d)
    for l in range(depth):
        proj = _in_proj(xs, attn_norm_w[l].reshape(1, d), w_in, l, tm=1024, tn=1024)
        o_part, qb, kb, tb = _rec_fwd(proj, lb_fwd, lb_bwd, l, d_hgrn)
        y_rec = _rec_bwd(qb, kb, tb, proj, o_part, hgrn_norm_w[l].reshape(1, d_hgrn), d_hgrn)
        y_conv = _conv(proj, conv_w[l], d_hgrn, d_conv, tc=512)
        xs = _out_proj(xs, y_rec, y_conv, w_out, l, tm=512)
        xs = _ffn(xs, ffn_norm_w[l].reshape(1, d), w_gate_up, w_down, fw, l, l == depth - 1,
                  tm=512, tf=512)
    return _deinterleave(xs)[None]
```

```python
import functools

import jax
import jax.numpy as jnp
from jax import lax
from jax.experimental import pallas as pl
from jax.experimental.pallas import tpu as pltpu

F32 = jnp.float32
BF16 = jnp.bfloat16

EPS = 1e-6
HEAD_DIM = 128
CHUNK = 128
SUBLANES = 8
LANES = 128
CHUNKS_PER_BLOCK = SUBLANES
TOK_BLOCK = CHUNK * CHUNKS_PER_BLOCK
POS_BLOCK = SUBLANES
N_GROUPS = CHUNK // POS_BLOCK
GROUP_ROWS = POS_BLOCK * CHUNKS_PER_BLOCK
LOW_LEVELS = tuple(1 << b for b in range(POS_BLOCK.bit_length() - 1))
HIGH_SPANS = tuple(1 << b for b in range(N_GROUPS.bit_length() - 1))
N_LOW = len(LOW_LEVELS) + 1
N_LEVELS = N_LOW + len(HIGH_SPANS)
N_FAC_LEVEL = 2 * (len(HIGH_SPANS) - 1)
FAC_QF, FAC_KF, FAC_QB, FAC_KB = (N_FAC_LEVEL + i for i in range(4))
FAC_RQF, FAC_RKF, FAC_RQB, FAC_RKB = (N_FAC_LEVEL + 4 + i for i in range(4))
N_FAC = N_FAC_LEVEL + 8
FAC_ROWS = N_FAC * N_GROUPS * CHUNKS_PER_BLOCK
BASE_QP, BASE_KS, BASE_QS, BASE_KP, BASE_QIS, BASE_KIP, BASE_QIP, BASE_KIS = range(8)
N_BASE = 8
RATIO_SPAN = 4
RATIO_MIN = 2.0 ** -100
BWD_BLOCKS = 4
GROUP_Q, GROUP_ZF, GROUP_ZB, GROUP_V, GROUP_GATE, GROUP_CB, GROUP_CC, GROUP_CH = range(8)
F32_GROUP0, N_F32_GROUPS = GROUP_ZF, 3
CONV_WIDTH = 3
HALO_ROWS = 2 * SUBLANES
VMEM_LIMIT = 52 * 1024 * 1024
CAST_BLOCK_BYTES = 8 * 1024 * 1024

_NT = (((1,), (1,)), ((), ()))
_TN = (((0,), (0,)), ((), ()))


def _sigmoid(x):
    return 1.0 / (1.0 + jnp.exp(-x))


def _rms(x, w):
    ms = jnp.mean(x * x, axis=-1, keepdims=True)
    return x * lax.rsqrt(ms + EPS) * w


def _params(sem):
    return pltpu.CompilerParams(dimension_semantics=sem, vmem_limit_bytes=VMEM_LIMIT)


def _cast_kernel(w_ref, o_ref):
    o_ref[...] = w_ref[...].astype(o_ref.dtype)


def _cast_bf16(w):
    depth, r, c = w.shape
    rows = depth * r
    tr = 1 << ((CAST_BLOCK_BYTES // (4 * c)).bit_length() - 1)
    assert tr >= 2 * SUBLANES and rows % tr == 0
    out = pl.pallas_call(
        _cast_kernel,
        grid=(rows // tr,),
        in_specs=[pl.BlockSpec((tr, c), lambda i: (i, 0))],
        out_specs=pl.BlockSpec((tr, c), lambda i: (i, 0)),
        out_shape=jax.ShapeDtypeStruct((rows, c), BF16),
        compiler_params=_params(("parallel",)),
        name="cast_bf16",
    )(w.reshape(rows, c))
    return out.reshape(depth, r, c)


def _in_proj_kernel(x_ref, nw_ref, w_ref, of_ref, oh_ref, h_scr):
    j = pl.program_id(1)

    @pl.when(j == 0)
    def _():
        h_scr[...] = _rms(x_ref[...], nw_ref[...]).astype(BF16)

    res = jnp.dot(h_scr[...], w_ref[...], preferred_element_type=F32)
    wide = jnp.logical_and(j >= F32_GROUP0, j < F32_GROUP0 + N_F32_GROUPS)

    @pl.when(wide)
    def _():
        for c in range(of_ref.shape[0]):
            of_ref[c] = res[:, c * LANES:(c + 1) * LANES]

    @pl.when(jnp.logical_not(wide))
    def _():
        for c in range(oh_ref.shape[0]):
            oh_ref[c] = res[:, c * LANES:(c + 1) * LANES].astype(oh_ref.dtype)


def _in_proj(x, nw, w, layer, *, tm, tn):
    s, d = x.shape
    n = w.shape[2]
    ngroups = n // tn
    slabs = tn // LANES
    assert F32_GROUP0 == 1 and ngroups > F32_GROUP0 + N_F32_GROUPS
    nhalf = ngroups - N_F32_GROUPS
    return pl.pallas_call(
        _in_proj_kernel,
        grid=(s // tm, ngroups),
        in_specs=[
            pl.BlockSpec((tm, d), lambda i, j: (i, 0)),
            pl.BlockSpec((1, d), lambda i, j: (0, 0)),
            pl.BlockSpec((None, d, tn), lambda i, j: (layer, 0, j)),
        ],
        out_specs=[
            pl.BlockSpec((slabs, tm, LANES),
                         lambda i, j: (jnp.clip(j - F32_GROUP0, 0, N_F32_GROUPS - 1), i, 0)),
            pl.BlockSpec((slabs, tm, LANES),
                         lambda i, j: (jnp.clip(j - N_F32_GROUPS, 0, nhalf - 1), i, 0)),
        ],
        out_shape=[jax.ShapeDtypeStruct((N_F32_GROUPS * slabs, s, LANES), F32),
                   jax.ShapeDtypeStruct((nhalf * slabs, s, LANES), BF16)],
        scratch_shapes=[pltpu.VMEM((tm, d), BF16)],
        compiler_params=_params(("parallel", "arbitrary")),
        name="in_proj",
    )(x, nw, w)


def _lower_bound(lb_ref, layer):
    p = lb_ref[...]
    if layer == 0:
        return jnp.zeros((1, p.shape[1]), F32)
    e = jnp.exp(p - jnp.max(p, axis=0, keepdims=True))
    sm = e / jnp.sum(e, axis=0, keepdims=True)
    return jnp.sum(sm[1:layer + 1], axis=0, keepdims=True)


def _gates(z, lb):
    t = jnp.exp(-jnp.abs(z))
    r = 1.0 / (1.0 + t)
    tr = t * r
    pos = z >= 0
    f = lb + (1.0 - lb) * jnp.where(pos, r, tr)
    k = (1.0 - lb) * jnp.where(pos, tr, r)
    return f, k


def _halves(x, h):
    x5 = x.reshape((x.shape[0] // (2 * h), 2, h) + x.shape[1:])
    return x5[:, 0], x5[:, 1]


def _join(a, b):
    x = jnp.stack([a, b], axis=1)
    return x.reshape((x.shape[0] * x.shape[1] * x.shape[2],) + a.shape[2:])


def _chunk_rows(ref, r):
    return ref[pl.ds(r, CHUNK, stride=CHUNKS_PER_BLOCK), :]


def _fac_rows(k, g):
    return pl.ds((k * N_GROUPS + g) * CHUNKS_PER_BLOCK, CHUNKS_PER_BLOCK)


def _group_factor(fac_ref, k, r):
    return jnp.concatenate(
        [jnp.broadcast_to(fac_ref[pl.ds((k * N_GROUPS + g) * CHUNKS_PER_BLOCK + r, 1), :],
                          (POS_BLOCK, HEAD_DIM))
         for g in range(N_GROUPS)], axis=0)


def _by_group_parity(even, odd, span):
    pieces = []
    for g in range(N_GROUPS):
        src = odd if (g // span) % 2 else even
        pieces.append(src[g * POS_BLOCK:(g + 1) * POS_BLOCK])
    return jnp.concatenate(pieces, axis=0)


def _seg_products(tot, span):
    ones = jnp.ones_like(tot[0])
    pre, suf = [None] * N_GROUPS, [None] * N_GROUPS
    for s0 in range(0, N_GROUPS, span):
        acc = None
        for g in range(s0, s0 + span):
            pre[g] = ones if acc is None else acc
            acc = tot[g] if acc is None else acc * tot[g]
        acc = None
        for g in reversed(range(s0, s0 + span)):
            suf[g] = ones if acc is None else acc
            acc = tot[g] if acc is None else acc * tot[g]
    return pre, suf


def _group_scan(g, q_ref, zf_ref, zb_ref, lbf, lbb, low_stage=None):
    sh = (POS_BLOCK, CHUNKS_PER_BLOCK, HEAD_DIM)
    flat = (GROUP_ROWS, HEAD_DIM)
    rows = pl.ds(g * GROUP_ROWS, GROUP_ROWS)
    q = q_ref[rows, :].astype(F32).reshape(sh)
    qs = q * _sigmoid(q)
    f_f, k_f = _gates(zf_ref[rows, :].reshape(sh), lbf)
    f_b, k_b = _gates(zb_ref[rows, :].reshape(sh), lbb)
    if low_stage is not None:
        stq_ref, stk_ref = low_stage
        stq_ref[0, rows, :] = qs.reshape(flat)
        stk_ref[0, rows, :] = (k_f + k_b).reshape(flat)
    pin, sex, tf = f_f, jnp.ones(sh, F32), f_f
    sin, pex, tb = f_b, jnp.ones(sh, F32), f_b
    for li, h in enumerate(LOW_LEVELS):
        pin1, pin2 = _halves(pin, h)
        sex1, sex2 = _halves(sex, h)
        sin1, sin2 = _halves(sin, h)
        pex1, pex2 = _halves(pex, h)
        if low_stage is not None:
            qs1, qs2 = _halves(qs, h)
            kf1, _ = _halves(k_f, h)
            _, kb2 = _halves(k_b, h)
            stq_ref[li + 1, rows, :] = _join(qs1 * sin1, qs2 * pin2).reshape(flat)
            stk_ref[li + 1, rows, :] = _join(kf1 * sex1, kb2 * pex2).reshape(flat)
        tf1, tf2 = _halves(tf, 1)
        tb1, tb2 = _halves(tb, 1)
        pin = _join(pin1, pin2 * tf1)
        sex = _join(sex1 * tf2, sex2)
        sin = _join(sin1 * tb2, sin2)
        pex = _join(pex1, pex2 * tb1)
        tf = (tf1 * tf2).reshape((tf.shape[0] // 2,) + sh[1:])
        tb = (tb1 * tb2).reshape((tb.shape[0] // 2,) + sh[1:])
    return rows, qs, k_f, k_b, pin, sex, sin, pex, tf[0], tb[0]


def _chunk_factors(tot_f, tot_b, fac_ref, tf_ref, tb_ref):
    pre_f, suf_f = _seg_products(tot_f, N_GROUPS)
    pre_b, suf_b = _seg_products(tot_b, N_GROUPS)
    for g in range(N_GROUPS):
        fac_ref[_fac_rows(FAC_QF, g), :] = pre_f[g]
        fac_ref[_fac_rows(FAC_KF, g), :] = suf_f[g]
        fac_ref[_fac_rows(FAC_QB, g), :] = suf_b[g]
        fac_ref[_fac_rows(FAC_KB, g), :] = pre_b[g]
    tf_ref[...] = pre_f[N_GROUPS - 1] * tot_f[N_GROUPS - 1]
    tb_ref[...] = pre_b[N_GROUPS - 1] * tot_b[N_GROUPS - 1]


def _level_factors(tot_f, tot_b, fac_ref, si):
    span = HIGH_SPANS[si]
    pre_f, suf_f = _seg_products(tot_f, span)
    pre_b, suf_b = _seg_products(tot_b, span)
    for g in range(N_GROUPS):
        second = (g // span) % 2
        fac_ref[_fac_rows(2 * (si - 1), g), :] = pre_f[g] if second else suf_b[g]
        fac_ref[_fac_rows(2 * (si - 1) + 1, g), :] = pre_b[g] if second else suf_f[g]


def _scan_ratio(layer, lbf_ref, lbb_ref, q_ref, zf_ref, zb_ref, tb_ref, stage):
    _, _, base_ref, fac_ref, tf_ref = stage
    lbf = _lower_bound(lbf_ref, layer)
    lbb = _lower_bound(lbb_ref, layer)
    flat = (GROUP_ROWS, HEAD_DIM)
    tot_f, tot_b = [], []
    for g in range(N_GROUPS):
        rows, qs, k_f, k_b, pin, sex, sin, pex, tf, tb = _group_scan(
            g, q_ref, zf_ref, zb_ref, lbf, lbb)
        base_ref[BASE_QP, rows, :] = (qs * pin).reshape(flat)
        base_ref[BASE_KS, rows, :] = (k_f * sex).reshape(flat)
        base_ref[BASE_QS, rows, :] = (qs * sin).reshape(flat)
        base_ref[BASE_KP, rows, :] = (k_b * pex).reshape(flat)
        base_ref[BASE_QIS, rows, :] = (qs * (1.0 / sex)).reshape(flat)
        base_ref[BASE_KIP, rows, :] = (k_f * (1.0 / pin)).reshape(flat)
        base_ref[BASE_QIP, rows, :] = (qs * (1.0 / pex)).reshape(flat)
        base_ref[BASE_KIS, rows, :] = (k_b * (1.0 / sin)).reshape(flat)
        tot_f.append(tf)
        tot_b.append(tb)

    pre_f, suf_f = _seg_products(tot_f, RATIO_SPAN)
    pre_b, suf_b = _seg_products(tot_b, RATIO_SPAN)
    smallest = None
    for g in range(N_GROUPS):
        second = (g // RATIO_SPAN) % 2
        fac_ref[_fac_rows(FAC_RQF, g), :] = pre_f[g] if second else 1.0 / suf_f[g]
        fac_ref[_fac_rows(FAC_RKF, g), :] = 1.0 / pre_f[g] if second else suf_f[g]
        fac_ref[_fac_rows(FAC_RQB, g), :] = 1.0 / pre_b[g] if second else suf_b[g]
        fac_ref[_fac_rows(FAC_RKB, g), :] = pre_b[g] if second else 1.0 / suf_b[g]
        if g % RATIO_SPAN == RATIO_SPAN - 1:
            run = jnp.minimum(pre_f[g] * tot_f[g], pre_b[g] * tot_b[g])
            smallest = run if smallest is None else jnp.minimum(smallest, run)
    _level_factors(tot_f, tot_b, fac_ref, len(HIGH_SPANS) - 1)
    _chunk_factors(tot_f, tot_b, fac_ref, tf_ref, tb_ref)
    return jnp.min(smallest)


def _scan_levels(layer, lbf_ref, lbb_ref, q_ref, zf_ref, zb_ref, tb_ref, stage):
    stq_ref, stk_ref, base_ref, fac_ref, tf_ref = stage
    lbf = _lower_bound(lbf_ref, layer)
    lbb = _lower_bound(lbb_ref, layer)
    flat = (GROUP_ROWS, HEAD_DIM)
    tot_f, tot_b = [], []
    for g in range(N_GROUPS):
        rows, qs, k_f, k_b, pin, sex, sin, pex, tf, tb = _group_scan(
            g, q_ref, zf_ref, zb_ref, lbf, lbb, (stq_ref, stk_ref))
        base_ref[BASE_QP, rows, :] = (qs * pin).reshape(flat)
        base_ref[BASE_KS, rows, :] = (k_f * sex).reshape(flat)
        base_ref[BASE_QS, rows, :] = (qs * sin).reshape(flat)
        base_ref[BASE_KP, rows, :] = (k_b * pex).reshape(flat)
        tot_f.append(tf)
        tot_b.append(tb)
    for si in range(1, len(HIGH_SPANS)):
        _level_factors(tot_f, tot_b, fac_ref, si)
    _chunk_factors(tot_f, tot_b, fac_ref, tf_ref, tb_ref)


def _chunk_bases(base_ref, r):
    return tuple(_chunk_rows(base_ref.at[k], r) for k in (BASE_QP, BASE_KS, BASE_QS, BASE_KP))


def _high_level(bases, fac_ref, si, r):
    qp, ks, qs, kp = bases
    span = HIGH_SPANS[si]
    ql = _by_group_parity(qs, qp, span)
    kl = _by_group_parity(ks, kp, span)
    if si > 0:
        ql = ql * _group_factor(fac_ref, 2 * (si - 1), r)
        kl = kl * _group_factor(fac_ref, 2 * (si - 1) + 1, r)
    return ql.astype(BF16), kl.astype(BF16)


def _nt_dot(a, b):
    return lax.dot_general(a, b, _NT, preferred_element_type=F32)


def _finish_chunk(r, a, bases, v_ref, o_ref, qb_ref, kb_ref, st_ref, fac_ref, tf_ref):
    qp, ks, qs, kp = bases
    vb = _chunk_rows(v_ref, r).astype(BF16)
    st = st_ref[...]
    o = jnp.dot(a.astype(BF16), vb, preferred_element_type=F32)
    qf = (qp * _group_factor(fac_ref, FAC_QF, r)).astype(BF16)
    o = o + _nt_dot(qf, st.astype(BF16))
    kf = (ks * _group_factor(fac_ref, FAC_KF, r)).astype(BF16)
    upd = lax.dot_general(vb, kf, _TN, preferred_element_type=F32)
    st_ref[...] = st * tf_ref[pl.ds(r, 1), :] + upd
    o_ref[pl.ds(r, CHUNK, stride=CHUNKS_PER_BLOCK), :] = o
    rows = pl.ds(r * CHUNK, CHUNK)
    qb_ref[rows, :] = (qs * _group_factor(fac_ref, FAC_QB, r)).astype(BF16)
    kb_ref[rows, :] = (kp * _group_factor(fac_ref, FAC_KB, r)).astype(BF16)


def _matmul_ratio(v_ref, o_ref, qb_ref, kb_ref, st_ref, stage):
    _, _, base_ref, fac_ref, tf_ref = stage
    row = lax.broadcasted_iota(jnp.int32, (CHUNK, CHUNK), 0)
    col = lax.broadcasted_iota(jnp.int32, (CHUNK, CHUNK), 1)
    far = (row ^ col) >= 2 * RATIO_SPAN * POS_BLOCK
    top = len(HIGH_SPANS) - 1
    for r in range(CHUNKS_PER_BLOCK):
        bases = _chunk_bases(base_ref, r)
        qp, ks, qs, kp = bases
        qis, kip, qip, kis = (_chunk_rows(base_ref.at[k], r)
                              for k in (BASE_QIS, BASE_KIP, BASE_QIP, BASE_KIS))
        qf = _by_group_parity(qis, qp, RATIO_SPAN) * _group_factor(fac_ref, FAC_RQF, r)
        kf = _by_group_parity(ks, kip, RATIO_SPAN) * _group_factor(fac_ref, FAC_RKF, r)
        qb = _by_group_parity(qs, qip, RATIO_SPAN) * _group_factor(fac_ref, FAC_RQB, r)
        kb = _by_group_parity(kis, kp, RATIO_SPAN) * _group_factor(fac_ref, FAC_RKB, r)
        p_f = _nt_dot(qf.astype(BF16), kf.astype(BF16))
        p_b = _nt_dot(qb.astype(BF16), kb.astype(BF16))
        p_top = _nt_dot(*_high_level(bases, fac_ref, top, r))
        a = jnp.where(far, p_top, jnp.where(row >= col, p_f, p_b))
        a = a + jnp.where(row == col, p_b, 0.0)
        _finish_chunk(r, a, bases, v_ref, o_ref, qb_ref, kb_ref, st_ref, fac_ref, tf_ref)


def _matmul_levels(v_ref, o_ref, qb_ref, kb_ref, st_ref, stage):
    stq_ref, stk_ref, base_ref, fac_ref, tf_ref = stage
    row = lax.broadcasted_iota(jnp.int32, (CHUNK, CHUNK), 0)
    col = lax.broadcasted_iota(jnp.int32, (CHUNK, CHUNK), 1)
    xr = row ^ col
    lvl = jnp.zeros((CHUNK, CHUNK), jnp.int32)
    for b in range(N_LEVELS - 1):
        lvl = lvl + jnp.where(xr >= (1 << b), 1, 0)

    for r in range(CHUNKS_PER_BLOCK):
        bases = _chunk_bases(base_ref, r)
        a = None
        for li in range(N_LEVELS):
            if li < N_LOW:
                ql = _chunk_rows(stq_ref.at[li], r).astype(BF16)
                kl = _chunk_rows(stk_ref.at[li], r).astype(BF16)
            else:
                ql, kl = _high_level(bases, fac_ref, li - N_LOW, r)
            p = _nt_dot(ql, kl)
            a = p if a is None else jnp.where(lvl == li, p, a)
        _finish_chunk(r, a, bases, v_ref, o_ref, qb_ref, kb_ref, st_ref, fac_ref, tf_ref)


def _rec_fwd_kernel(layer, lbf_ref, lbb_ref, q_ref, zf_ref, zb_ref, v_ref,
                    o_ref, qb_ref, kb_ref, tb_ref, *scratch):
    stage, st_ref = scratch[:-1], scratch[-1]

    @pl.when(pl.program_id(1) == 0)
    def _():
        st_ref[...] = jnp.zeros_like(st_ref)

    smallest = _scan_ratio(layer, lbf_ref, lbb_ref, q_ref, zf_ref, zb_ref, tb_ref, stage)
    safe = smallest >= RATIO_MIN

    @pl.when(safe)
    def _():
        _matmul_ratio(v_ref, o_ref, qb_ref, kb_ref, st_ref, stage)

    @pl.when(jnp.logical_not(safe))
    def _():
        _scan_levels(layer, lbf_ref, lbb_ref, q_ref, zf_ref, zb_ref, tb_ref, stage)
        _matmul_levels(v_ref, o_ref, qb_ref, kb_ref, st_ref, stage)


def _group_index(group):
    if F32_GROUP0 <= group < F32_GROUP0 + N_F32_GROUPS:
        return group - F32_GROUP0
    return group if group < F32_GROUP0 else group - N_F32_GROUPS


def _rec_fwd(proj_f, proj_h, lb_fwd, lb_bwd, layer, d_hgrn):
    s = proj_f.shape[1]
    depth = lb_fwd.shape[0]
    nh = d_hgrn // HEAD_DIM
    nblk = s // TOK_BLOCK
    blk = (TOK_BLOCK, HEAD_DIM)
    hblk = (None,) + blk
    here = lambda h, i: (h, i, 0)

    def col(group):
        return pl.BlockSpec(hblk, lambda h, i, g=_group_index(group): (g * nh + h, i, 0))

    lb_spec = pl.BlockSpec((depth, HEAD_DIM), lambda h, i: (0, h))
    return pl.pallas_call(
        functools.partial(_rec_fwd_kernel, layer),
        grid=(nh, nblk),
        in_specs=[lb_spec, lb_spec, col(GROUP_Q), col(GROUP_ZF), col(GROUP_ZB), col(GROUP_V)],
        out_specs=[
            pl.BlockSpec(hblk, here),
            pl.BlockSpec(hblk, here),
            pl.BlockSpec(hblk, here),
            pl.BlockSpec((None, CHUNKS_PER_BLOCK, HEAD_DIM), here),
        ],
        out_shape=[jax.ShapeDtypeStruct((nh, s, HEAD_DIM), F32),
                   jax.ShapeDtypeStruct((nh, s, HEAD_DIM), BF16),
                   jax.ShapeDtypeStruct((nh, s, HEAD_DIM), BF16),
                   jax.ShapeDtypeStruct((nh, nblk * CHUNKS_PER_BLOCK, HEAD_DIM), F32)],
        scratch_shapes=[
            pltpu.VMEM((N_LOW,) + blk, F32),
            pltpu.VMEM((N_LOW,) + blk, F32),
            pltpu.VMEM((N_BASE,) + blk, F32),
            pltpu.VMEM((FAC_ROWS, HEAD_DIM), F32),
            pltpu.VMEM((CHUNKS_PER_BLOCK, HEAD_DIM), F32),
            pltpu.VMEM((HEAD_DIM, HEAD_DIM), F32),
        ],
        compiler_params=_params(("parallel", "arbitrary")),
        name="rec_fwd",
    )(lb_fwd, lb_bwd, proj_h, proj_f, proj_f, proj_f)


def _rec_bwd_kernel(qb_ref, kb_ref, tb_ref, v_ref, op_ref, g_ref, nw_ref, y_ref, ob_ref, st_ref):
    @pl.when(pl.program_id(1) == 0)
    def _():
        st_ref[...] = jnp.zeros_like(st_ref)

    for blk in reversed(range(BWD_BLOCKS)):
        for r in reversed(range(CHUNKS_PER_BLOCK)):
            st = st_ref[...]
            rows = pl.ds(blk * TOK_BLOCK + r * CHUNK, CHUNK)
            strided = pl.ds(blk * TOK_BLOCK + r, CHUNK, stride=CHUNKS_PER_BLOCK)
            ob_ref[strided, :] = _nt_dot(qb_ref[rows, :], st.astype(BF16))
            upd = lax.dot_general(v_ref[strided, :].astype(BF16), kb_ref[rows, :], _TN,
                                  preferred_element_type=F32)
            st_ref[...] = st * tb_ref[pl.ds(blk * CHUNKS_PER_BLOCK + r, 1), :] + upd
    g = g_ref[...].astype(F32)
    y = _rms(op_ref[...] + ob_ref[...], nw_ref[...]) * (g * _sigmoid(g))
    y_ref[...] = y.astype(y_ref.dtype)


def _rec_bwd(qb, kb, tb, proj_f, proj_h, o_part, norm_w, d_hgrn):
    s = proj_f.shape[1]
    nh = d_hgrn // HEAD_DIM
    rows = BWD_BLOCKS * TOK_BLOCK
    nstep = s // rows
    blk = (rows, HEAD_DIM)
    hblk = (None,) + blk
    rev = lambda h, i: (h, nstep - 1 - i, 0)

    def col(group):
        return pl.BlockSpec(hblk, lambda h, i, g=_group_index(group): (g * nh + h, nstep - 1 - i, 0))

    return pl.pallas_call(
        _rec_bwd_kernel,
        grid=(nh, nstep),
        in_specs=[
            pl.BlockSpec(hblk, rev),
            pl.BlockSpec(hblk, rev),
            pl.BlockSpec((None, BWD_BLOCKS * CHUNKS_PER_BLOCK, HEAD_DIM), rev),
            col(GROUP_V),
            pl.BlockSpec(hblk, rev),
            col(GROUP_GATE),
            pl.BlockSpec((1, HEAD_DIM), lambda h, i: (0, h)),
        ],
        out_specs=pl.BlockSpec(blk, lambda h, i: (nstep - 1 - i, h)),
        out_shape=jax.ShapeDtypeStruct((s, d_hgrn), BF16),
        scratch_shapes=[pltpu.VMEM(blk, F32), pltpu.VMEM((HEAD_DIM, HEAD_DIM), F32)],
        compiler_params=_params(("parallel", "arbitrary")),
        name="rec_bwd",
    )(qb, kb, tb, proj_f, o_part, proj_h, norm_w)


def _conv_kernel(b_ref, c_ref, h_ref, cp_ref, hp_ref, cn_ref, hn_ref, w_ref, y_ref):
    i = pl.program_id(0)
    last = pl.num_programs(0) - 1
    n = c_ref.shape[1]
    sub = lax.broadcasted_iota(jnp.int32, (SUBLANES, LANES), 0)
    for c in range(c_ref.shape[0]):
        u = c_ref[c].astype(F32) * h_ref[c].astype(F32)
        halo_p = (cp_ref[c].astype(F32) * hp_ref[c].astype(F32))[HALO_ROWS - SUBLANES:]
        halo_p = jnp.where(i > 0, halo_p, 0.0)
        first = jnp.where(sub == 0, pltpu.roll(halo_p, 1, 0), pltpu.roll(u[n - SUBLANES:], 1, 0))
        u_prev = jnp.concatenate([first, u[:n - SUBLANES]], axis=0)
        halo_n = (cn_ref[c].astype(F32) * hn_ref[c].astype(F32))[:SUBLANES]
        halo_n = jnp.where(i < last, halo_n, 0.0)
        lastv = jnp.where(sub == SUBLANES - 1, pltpu.roll(halo_n, SUBLANES - 1, 0),
                          pltpu.roll(u[:SUBLANES], SUBLANES - 1, 0))
        u_next = jnp.concatenate([u[SUBLANES:], lastv], axis=0)
        w = w_ref[:, c * LANES:(c + 1) * LANES]
        y = w[0:1] * u_prev + w[1:2] * u + w[2:3] * u_next
        y_ref[:, c * LANES:(c + 1) * LANES] = (b_ref[c].astype(F32) * y).astype(y_ref.dtype)


def _conv(proj_h, conv_w, d_hgrn, d_conv, *, tc):
    s = proj_h.shape[1]
    nblk = s // TOK_BLOCK
    nhalo = s // HALO_ROWS
    per_blk = TOK_BLOCK // HALO_ROWS
    slabs = tc // LANES
    ncol = d_conv // tc

    def first_block(group):
        return _group_index(group) * (d_hgrn // tc)

    def main(group):
        return pl.BlockSpec((slabs, TOK_BLOCK, LANES),
                            lambda i, j, b=first_block(group): (b + j, i, 0))

    def prev(group):
        return pl.BlockSpec((slabs, HALO_ROWS, LANES),
                            lambda i, j, b=first_block(group): (b + j,
                                                                jnp.maximum(i * per_blk - 1, 0), 0))

    def nxt(group):
        return pl.BlockSpec((slabs, HALO_ROWS, LANES),
                            lambda i, j, b=first_block(group): (
                                b + j, jnp.minimum((i + 1) * per_blk, nhalo - 1), 0))

    return pl.pallas_call(
        _conv_kernel,
        grid=(nblk, ncol),
        in_specs=[main(GROUP_CB), main(GROUP_CC), main(GROUP_CH), prev(GROUP_CC), prev(GROUP_CH),
                  nxt(GROUP_CC), nxt(GROUP_CH),
                  pl.BlockSpec((CONV_WIDTH, tc), lambda i, j: (0, j))],
        out_specs=pl.BlockSpec((TOK_BLOCK, tc), lambda i, j: (i, j)),
        out_shape=jax.ShapeDtypeStruct((s, d_conv), BF16),
        compiler_params=_params(("parallel", "parallel")),
        name="short_conv",
    )(proj_h, proj_h, proj_h, proj_h, proj_h, proj_h, proj_h, conv_w)


def _out_proj_kernel(x_ref, yr_ref, yc_ref, wr_ref, wc_ref, o_ref):
    acc = jnp.dot(yr_ref[...], wr_ref[...], preferred_element_type=F32)
    acc = acc + jnp.dot(yc_ref[...], wc_ref[...], preferred_element_type=F32)
    o_ref[...] = x_ref[...] + acc


def _out_proj(x, y_rec, y_conv, w, layer, *, tm):
    s, d = x.shape
    d_rec = y_rec.shape[1]
    d_conv = y_conv.shape[1]
    assert d_rec == d_conv
    return pl.pallas_call(
        _out_proj_kernel,
        grid=(s // tm,),
        in_specs=[
            pl.BlockSpec((tm, d), lambda i: (i, 0)),
            pl.BlockSpec((tm, d_rec), lambda i: (i, 0)),
            pl.BlockSpec((tm, d_conv), lambda i: (i, 0)),
            pl.BlockSpec((None, d_rec, d), lambda i: (layer, 0, 0)),
            pl.BlockSpec((None, d_conv, d), lambda i: (layer, 1, 0)),
        ],
        out_specs=pl.BlockSpec((tm, d), lambda i: (i, 0)),
        out_shape=jax.ShapeDtypeStruct((s, d), F32),
        compiler_params=_params(("parallel",)),
        name="out_proj",
    )(x, y_rec, y_conv, w, w)


def _ffn_kernel(final, x_ref, nw_ref, wg_ref, wu_ref, wd_ref, fw_ref, o_ref, h_scr):
    f = pl.program_id(1)

    @pl.when(f == 0)
    def _():
        x = x_ref[...]
        h_scr[...] = _rms(x, nw_ref[...]).astype(BF16)
        o_ref[...] = x

    h = h_scr[...]
    g = jnp.dot(h, wg_ref[...], preferred_element_type=F32)
    u = jnp.dot(h, wu_ref[...], preferred_element_type=F32)
    a = (g * _sigmoid(g) * u).astype(BF16)
    o_ref[...] += jnp.dot(a, wd_ref[...], preferred_element_type=F32)

    if final:
        @pl.when(f == pl.num_programs(1) - 1)
        def _():
            o_ref[...] = _rms(o_ref[...], fw_ref[...])


def _ffn(x, nw, w_gu, w_d, fw, layer, final, *, tm, tf):
    s, d = x.shape
    d_ff = w_d.shape[1]
    nf = d_ff // tf
    return pl.pallas_call(
        functools.partial(_ffn_kernel, final),
        grid=(s // tm, nf),
        in_specs=[
            pl.BlockSpec((tm, d), lambda i, f: (i, 0)),
            pl.BlockSpec((1, d), lambda i, f: (0, 0)),
            pl.BlockSpec((None, d, tf), lambda i, f: (layer, 0, f)),
            pl.BlockSpec((None, d, tf), lambda i, f: (layer, 0, nf + f)),
            pl.BlockSpec((None, tf, d), lambda i, f: (layer, f, 0)),
            pl.BlockSpec((1, d), lambda i, f: (0, 0)),
        ],
        out_specs=pl.BlockSpec((tm, d), lambda i, f: (i, 0)),
        out_shape=jax.ShapeDtypeStruct((s, d), F32),
        scratch_shapes=[pltpu.VMEM((tm, d), BF16)],
        compiler_params=_params(("parallel", "arbitrary")),
        name="ffn",
    )(x, nw, w_gu, w_gu, w_d, fw)


def _interleave(x):
    s, d = x.shape
    x4 = x.reshape(s // TOK_BLOCK, CHUNKS_PER_BLOCK, CHUNK, d)
    return x4.transpose(0, 2, 1, 3).reshape(s, d)


def _deinterleave(x):
    s, d = x.shape
    x4 = x.reshape(s // TOK_BLOCK, CHUNK, CHUNKS_PER_BLOCK, d)
    return x4.transpose(0, 2, 1, 3).reshape(s, d)


def kernel(x, attn_norm_w, w_in, lb_fwd, lb_bwd, hgrn_norm_w, conv_w, w_out, ffn_norm_w,
           w_gate_up, w_down, final_norm_w):
    bsz, s, d = x.shape
    depth = w_in.shape[0]
    d_hgrn = lb_fwd.shape[1]
    d_conv = conv_w.shape[2]
    assert bsz == 1 and s % (BWD_BLOCKS * TOK_BLOCK) == 0
    assert w_in.shape[2] == 5 * d_hgrn + 3 * d_conv and d_hgrn == d_conv

    xs = _interleave(x[0])
    w_in, w_out, w_gate_up, w_down = (_cast_bf16(w) for w in (w_in, w_out, w_gate_up, w_down))
    fw = final_norm_w.reshape(1, d)
    for l in range(depth):
        proj_f, proj_h = _in_proj(xs, attn_norm_w[l].reshape(1, d), w_in, l, tm=1024, tn=d_hgrn)
        o_part, qb, kb, tb = _rec_fwd(proj_f, proj_h, lb_fwd, lb_bwd, l, d_hgrn)
        y_rec = _rec_bwd(qb, kb, tb, proj_f, proj_h, o_part, hgrn_norm_w[l].reshape(1, d_hgrn),
                         d_hgrn)
        y_conv = _conv(proj_h, conv_w[l], d_hgrn, d_conv, tc=512)
        xs = _out_proj(xs, y_rec, y_conv, w_out, l, tm=512)
        xs = _ffn(xs, ffn_norm_w[l].reshape(1, d), w_gate_up, w_down, fw, l, l == depth - 1,
                  tm=512, tf=512)
    return _deinterleave(xs)[None]
```

```python
import functools

import jax
import jax.numpy as jnp
from jax import lax
from jax.experimental import pallas as pl
from jax.experimental.pallas import tpu as pltpu

F32 = jnp.float32
BF16 = jnp.bfloat16

EPS = 1e-6
HEAD_DIM = 128
CHUNK = 128
SUBLANES = 8
LANES = 128
CHUNKS_PER_BLOCK = SUBLANES
TOK_BLOCK = CHUNK * CHUNKS_PER_BLOCK
POS_BLOCK = SUBLANES
N_GROUPS = CHUNK // POS_BLOCK
GROUP_ROWS = POS_BLOCK * CHUNKS_PER_BLOCK
LOW_LEVELS = tuple(1 << b for b in range(POS_BLOCK.bit_length() - 1))
HIGH_SPANS = tuple(1 << b for b in range(N_GROUPS.bit_length() - 1))
N_LOW = len(LOW_LEVELS) + 1
N_LEVELS = N_LOW + len(HIGH_SPANS)
N_FAC_LEVEL = 2 * (len(HIGH_SPANS) - 1)
FAC_QF, FAC_KF, FAC_QB, FAC_KB = (N_FAC_LEVEL + i for i in range(4))
FAC_RQF, FAC_RKF, FAC_RQB, FAC_RKB = (N_FAC_LEVEL + 4 + i for i in range(4))
N_FAC = N_FAC_LEVEL + 8
FAC_ROWS = N_FAC * N_GROUPS * CHUNKS_PER_BLOCK
BASE_QP, BASE_KS, BASE_QS, BASE_KP, BASE_QIS, BASE_KIP, BASE_QIP, BASE_KIS = range(8)
N_BASE = 8
RATIO_SPAN = 4
RATIO_MIN = 2.0 ** -100
BWD_BLOCKS = 4
GROUP_Q, GROUP_ZF, GROUP_ZB, GROUP_V, GROUP_GATE, GROUP_CB, GROUP_CC, GROUP_CH = range(8)
F32_GROUP0, N_F32_GROUPS = GROUP_ZF, 3
CONV_WIDTH = 3
HALO_ROWS = 2 * SUBLANES
VMEM_LIMIT = 52 * 1024 * 1024
CAST_BLOCK_BYTES = 8 * 1024 * 1024

_NT = (((1,), (1,)), ((), ()))
_TN = (((0,), (0,)), ((), ()))


def _sigmoid(x):
    return 1.0 / (1.0 + jnp.exp(-x))


def _rms(x, w):
    ms = jnp.mean(x * x, axis=-1, keepdims=True)
    return x * lax.rsqrt(ms + EPS) * w


def _params(sem):
    return pltpu.CompilerParams(dimension_semantics=sem, vmem_limit_bytes=VMEM_LIMIT)


def _cast_kernel(w_ref, o_ref):
    o_ref[...] = w_ref[...].astype(o_ref.dtype)


def _cast_bf16(w):
    depth, r, c = w.shape
    rows = depth * r
    tr = 1 << ((CAST_BLOCK_BYTES // (4 * c)).bit_length() - 1)
    assert tr >= 2 * SUBLANES and rows % tr == 0
    out = pl.pallas_call(
        _cast_kernel,
        grid=(rows // tr,),
        in_specs=[pl.BlockSpec((tr, c), lambda i: (i, 0))],
        out_specs=pl.BlockSpec((tr, c), lambda i: (i, 0)),
        out_shape=jax.ShapeDtypeStruct((rows, c), BF16),
        compiler_params=_params(("parallel",)),
        name="cast_bf16",
    )(w.reshape(rows, c))
    return out.reshape(depth, r, c)


def _in_proj_kernel(x_ref, nw_ref, w_ref, of_ref, oh_ref, h_scr):
    j = pl.program_id(1)

    @pl.when(j == 0)
    def _():
        h_scr[...] = _rms(x_ref[...], nw_ref[...]).astype(BF16)

    def project(o_ref):
        res = jnp.dot(h_scr[...], w_ref[...], preferred_element_type=F32)
        for c in range(o_ref.shape[0]):
            o_ref[c] = res[:, c * LANES:(c + 1) * LANES].astype(o_ref.dtype)

    wide = jnp.logical_and(j >= F32_GROUP0, j < F32_GROUP0 + N_F32_GROUPS)
    pl.when(wide)(functools.partial(project, of_ref))
    pl.when(jnp.logical_not(wide))(functools.partial(project, oh_ref))


def _in_proj(x, nw, w, layer, *, tm, tn):
    s, d = x.shape
    n = w.shape[2]
    ngroups = n // tn
    slabs = tn // LANES
    assert F32_GROUP0 == 1 and ngroups > F32_GROUP0 + N_F32_GROUPS
    nhalf = ngroups - N_F32_GROUPS
    return pl.pallas_call(
        _in_proj_kernel,
        grid=(s // tm, ngroups),
        in_specs=[
            pl.BlockSpec((tm, d), lambda i, j: (i, 0)),
            pl.BlockSpec((1, d), lambda i, j: (0, 0)),
            pl.BlockSpec((None, d, tn), lambda i, j: (layer, 0, j)),
        ],
        out_specs=[
            pl.BlockSpec((slabs, tm, LANES),
                         lambda i, j: (jnp.clip(j - F32_GROUP0, 0, N_F32_GROUPS - 1), i, 0)),
            pl.BlockSpec((slabs, tm, LANES),
                         lambda i, j: (jnp.clip(j - N_F32_GROUPS, 0, nhalf - 1), i, 0)),
        ],
        out_shape=[jax.ShapeDtypeStruct((N_F32_GROUPS * slabs, s, LANES), F32),
                   jax.ShapeDtypeStruct((nhalf * slabs, s, LANES), BF16)],
        scratch_shapes=[pltpu.VMEM((tm, d), BF16)],
        compiler_params=_params(("parallel", "arbitrary")),
        name="in_proj",
    )(x, nw, w)


def _lower_bound(lb_ref, layer):
    p = lb_ref[...]
    if layer == 0:
        return jnp.zeros((1, p.shape[1]), F32)
    e = jnp.exp(p - jnp.max(p, axis=0, keepdims=True))
    sm = e / jnp.sum(e, axis=0, keepdims=True)
    return jnp.sum(sm[1:layer + 1], axis=0, keepdims=True)


def _gates(z, lb):
    t = jnp.exp(-jnp.abs(z))
    r = 1.0 / (1.0 + t)
    tr = t * r
    pos = z >= 0
    f = lb + (1.0 - lb) * jnp.where(pos, r, tr)
    k = (1.0 - lb) * jnp.where(pos, tr, r)
    return f, k


def _halves(x, h):
    x5 = x.reshape((x.shape[0] // (2 * h), 2, h) + x.shape[1:])
    return x5[:, 0], x5[:, 1]


def _join(a, b):
    x = jnp.stack([a, b], axis=1)
    return x.reshape((x.shape[0] * x.shape[1] * x.shape[2],) + a.shape[2:])


def _chunk_rows(ref, r):
    return ref[pl.ds(r, CHUNK, stride=CHUNKS_PER_BLOCK), :]


def _fac_rows(k, g):
    return pl.ds((k * N_GROUPS + g) * CHUNKS_PER_BLOCK, CHUNKS_PER_BLOCK)


def _group_factor(fac_ref, k, r):
    return jnp.concatenate(
        [jnp.broadcast_to(fac_ref[pl.ds((k * N_GROUPS + g) * CHUNKS_PER_BLOCK + r, 1), :],
                          (POS_BLOCK, HEAD_DIM))
         for g in range(N_GROUPS)], axis=0)


def _by_group_parity(even, odd, span):
    pieces = []
    for g in range(N_GROUPS):
        src = odd if (g // span) % 2 else even
        pieces.append(src[g * POS_BLOCK:(g + 1) * POS_BLOCK])
    return jnp.concatenate(pieces, axis=0)


def _seg_products(tot, span):
    ones = jnp.ones_like(tot[0])
    pre, suf = [None] * N_GROUPS, [None] * N_GROUPS
    for s0 in range(0, N_GROUPS, span):
        acc = None
        for g in range(s0, s0 + span):
            pre[g] = ones if acc is None else acc
            acc = tot[g] if acc is None else acc * tot[g]
        acc = None
        for g in reversed(range(s0, s0 + span)):
            suf[g] = ones if acc is None else acc
            acc = tot[g] if acc is None else acc * tot[g]
    return pre, suf


def _group_scan(g, q_ref, zf_ref, zb_ref, lbf, lbb, low_stage=None):
    sh = (POS_BLOCK, CHUNKS_PER_BLOCK, HEAD_DIM)
    flat = (GROUP_ROWS, HEAD_DIM)
    rows = pl.ds(g * GROUP_ROWS, GROUP_ROWS)
    q = q_ref[rows, :].astype(F32).reshape(sh)
    qs = q * _sigmoid(q)
    f_f, k_f = _gates(zf_ref[rows, :].reshape(sh), lbf)
    f_b, k_b = _gates(zb_ref[rows, :].reshape(sh), lbb)
    if low_stage is not None:
        stq_ref, stk_ref = low_stage
        stq_ref[0, rows, :] = qs.reshape(flat)
        stk_ref[0, rows, :] = (k_f + k_b).reshape(flat)
    pin, sex, tf = f_f, jnp.ones(sh, F32), f_f
    sin, pex, tb = f_b, jnp.ones(sh, F32), f_b
    for li, h in enumerate(LOW_LEVELS):
        pin1, pin2 = _halves(pin, h)
        sex1, sex2 = _halves(sex, h)
        sin1, sin2 = _halves(sin, h)
        pex1, pex2 = _halves(pex, h)
        if low_stage is not None:
            qs1, qs2 = _halves(qs, h)
            kf1, _ = _halves(k_f, h)
            _, kb2 = _halves(k_b, h)
            stq_ref[li + 1, rows, :] = _join(qs1 * sin1, qs2 * pin2).reshape(flat)
            stk_ref[li + 1, rows, :] = _join(kf1 * sex1, kb2 * pex2).reshape(flat)
        tf1, tf2 = _halves(tf, 1)
        tb1, tb2 = _halves(tb, 1)
        pin = _join(pin1, pin2 * tf1)
        sex = _join(sex1 * tf2, sex2)
        sin = _join(sin1 * tb2, sin2)
        pex = _join(pex1, pex2 * tb1)
        tf = (tf1 * tf2).reshape((tf.shape[0] // 2,) + sh[1:])
        tb = (tb1 * tb2).reshape((tb.shape[0] // 2,) + sh[1:])
    return rows, qs, k_f, k_b, pin, sex, sin, pex, tf[0], tb[0]


def _chunk_factors(tot_f, tot_b, fac_ref, tf_ref, tb_ref):
    pre_f, suf_f = _seg_products(tot_f, N_GROUPS)
    pre_b, suf_b = _seg_products(tot_b, N_GROUPS)
    for g in range(N_GROUPS):
        fac_ref[_fac_rows(FAC_QF, g), :] = pre_f[g]
        fac_ref[_fac_rows(FAC_KF, g), :] = suf_f[g]
        fac_ref[_fac_rows(FAC_QB, g), :] = suf_b[g]
        fac_ref[_fac_rows(FAC_KB, g), :] = pre_b[g]
    tf_ref[...] = pre_f[N_GROUPS - 1] * tot_f[N_GROUPS - 1]
    tb_ref[...] = pre_b[N_GROUPS - 1] * tot_b[N_GROUPS - 1]


def _level_factors(tot_f, tot_b, fac_ref, si):
    span = HIGH_SPANS[si]
    pre_f, suf_f = _seg_products(tot_f, span)
    pre_b, suf_b = _seg_products(tot_b, span)
    for g in range(N_GROUPS):
        second = (g // span) % 2
        fac_ref[_fac_rows(2 * (si - 1), g), :] = pre_f[g] if second else suf_b[g]
        fac_ref[_fac_rows(2 * (si - 1) + 1, g), :] = pre_b[g] if second else suf_f[g]


def _scan_ratio(layer, lbf_ref, lbb_ref, q_ref, zf_ref, zb_ref, tb_ref, stage):
    _, _, base_ref, fac_ref, tf_ref = stage
    lbf = _lower_bound(lbf_ref, layer)
    lbb = _lower_bound(lbb_ref, layer)
    flat = (GROUP_ROWS, HEAD_DIM)
    tot_f, tot_b = [], []
    for g in range(N_GROUPS):
        rows, qs, k_f, k_b, pin, sex, sin, pex, tf, tb = _group_scan(
            g, q_ref, zf_ref, zb_ref, lbf, lbb)
        base_ref[BASE_QP, rows, :] = (qs * pin).reshape(flat)
        base_ref[BASE_KS, rows, :] = (k_f * sex).reshape(flat)
        base_ref[BASE_QS, rows, :] = (qs * sin).reshape(flat)
        base_ref[BASE_KP, rows, :] = (k_b * pex).reshape(flat)
        base_ref[BASE_QIS, rows, :] = (qs * (1.0 / sex)).reshape(flat)
        base_ref[BASE_KIP, rows, :] = (k_f * (1.0 / pin)).reshape(flat)
        base_ref[BASE_QIP, rows, :] = (qs * (1.0 / pex)).reshape(flat)
        base_ref[BASE_KIS, rows, :] = (k_b * (1.0 / sin)).reshape(flat)
        tot_f.append(tf)
        tot_b.append(tb)

    pre_f, suf_f = _seg_products(tot_f, RATIO_SPAN)
    pre_b, suf_b = _seg_products(tot_b, RATIO_SPAN)
    smallest = None
    for g in range(N_GROUPS):
        second = (g // RATIO_SPAN) % 2
        fac_ref[_fac_rows(FAC_RQF, g), :] = pre_f[g] if second else 1.0 / suf_f[g]
        fac_ref[_fac_rows(FAC_RKF, g), :] = 1.0 / pre_f[g] if second else suf_f[g]
        fac_ref[_fac_rows(FAC_RQB, g), :] = 1.0 / pre_b[g] if second else suf_b[g]
        fac_ref[_fac_rows(FAC_RKB, g), :] = pre_b[g] if second else 1.0 / suf_b[g]
        if g % RATIO_SPAN == RATIO_SPAN - 1:
            run = jnp.minimum(pre_f[g] * tot_f[g], pre_b[g] * tot_b[g])
            smallest = run if smallest is None else jnp.minimum(smallest, run)
    _level_factors(tot_f, tot_b, fac_ref, len(HIGH_SPANS) - 1)
    _chunk_factors(tot_f, tot_b, fac_ref, tf_ref, tb_ref)
    return jnp.min(smallest)


def _scan_levels(layer, lbf_ref, lbb_ref, q_ref, zf_ref, zb_ref, tb_ref, stage):
    stq_ref, stk_ref, base_ref, fac_ref, tf_ref = stage
    lbf = _lower_bound(lbf_ref, layer)
    lbb = _lower_bound(lbb_ref, layer)
    flat = (GROUP_ROWS, HEAD_DIM)
    tot_f, tot_b = [], []
    for g in range(N_GROUPS):
        rows, qs, k_f, k_b, pin, sex, sin, pex, tf, tb = _group_scan(
            g, q_ref, zf_ref, zb_ref, lbf, lbb, (stq_ref, stk_ref))
        base_ref[BASE_QP, rows, :] = (qs * pin).reshape(flat)
        base_ref[BASE_KS, rows, :] = (k_f * sex).reshape(flat)
        base_ref[BASE_QS, rows, :] = (qs * sin).reshape(flat)
        base_ref[BASE_KP, rows, :] = (k_b * pex).reshape(flat)
        tot_f.append(tf)
        tot_b.append(tb)
    for si in range(1, len(HIGH_SPANS)):
        _level_factors(tot_f, tot_b, fac_ref, si)
    _chunk_factors(tot_f, tot_b, fac_ref, tf_ref, tb_ref)


def _chunk_bases(base_ref, r):
    return tuple(_chunk_rows(base_ref.at[k], r) for k in (BASE_QP, BASE_KS, BASE_QS, BASE_KP))


def _high_level(bases, fac_ref, si, r):
    qp, ks, qs, kp = bases
    span = HIGH_SPANS[si]
    ql = _by_group_parity(qs, qp, span)
    kl = _by_group_parity(ks, kp, span)
    if si > 0:
        ql = ql * _group_factor(fac_ref, 2 * (si - 1), r)
        kl = kl * _group_factor(fac_ref, 2 * (si - 1) + 1, r)
    return ql.astype(BF16), kl.astype(BF16)


def _nt_dot(a, b):
    return lax.dot_general(a, b, _NT, preferred_element_type=F32)


def _finish_chunk(r, a, bases, v_ref, o_ref, bwd_refs, st_ref, fac_ref, tf_ref):
    qp, ks, qs, kp = bases
    vb = _chunk_rows(v_ref, r).astype(BF16)
    st = st_ref[...]
    o = jnp.dot(a.astype(BF16), vb, preferred_element_type=F32)
    qf = (qp * _group_factor(fac_ref, FAC_QF, r)).astype(BF16)
    o = o + _nt_dot(qf, st.astype(BF16))
    kf = (ks * _group_factor(fac_ref, FAC_KF, r)).astype(BF16)
    upd = lax.dot_general(vb, kf, _TN, preferred_element_type=F32)
    st_ref[...] = st * tf_ref[pl.ds(r, 1), :] + upd
    o_ref[pl.ds(r, CHUNK, stride=CHUNKS_PER_BLOCK), :] = o
    rows = pl.ds(r * CHUNK, CHUNK)
    qb_ref, kb_ref, vb_ref = bwd_refs
    qb_ref[rows, :] = (qs * _group_factor(fac_ref, FAC_QB, r)).astype(BF16)
    kb_ref[rows, :] = (kp * _group_factor(fac_ref, FAC_KB, r)).astype(BF16)
    vb_ref[rows, :] = vb


def _matmul_ratio(v_ref, o_ref, bwd_refs, st_ref, stage):
    _, _, base_ref, fac_ref, tf_ref = stage
    row = lax.broadcasted_iota(jnp.int32, (CHUNK, CHUNK), 0)
    col = lax.broadcasted_iota(jnp.int32, (CHUNK, CHUNK), 1)
    far = (row ^ col) >= 2 * RATIO_SPAN * POS_BLOCK
    top = len(HIGH_SPANS) - 1
    for r in range(CHUNKS_PER_BLOCK):
        bases = _chunk_bases(base_ref, r)
        qp, ks, qs, kp = bases
        qis, kip, qip, kis = (_chunk_rows(base_ref.at[k], r)
                              for k in (BASE_QIS, BASE_KIP, BASE_QIP, BASE_KIS))
        qf = _by_group_parity(qis, qp, RATIO_SPAN) * _group_factor(fac_ref, FAC_RQF, r)
        kf = _by_group_parity(ks, kip, RATIO_SPAN) * _group_factor(fac_ref, FAC_RKF, r)
        qb = _by_group_parity(qs, qip, RATIO_SPAN) * _group_factor(fac_ref, FAC_RQB, r)
        kb = _by_group_parity(kis, kp, RATIO_SPAN) * _group_factor(fac_ref, FAC_RKB, r)
        p_f = _nt_dot(qf.astype(BF16), kf.astype(BF16))
        p_b = _nt_dot(qb.astype(BF16), kb.astype(BF16))
        p_top = _nt_dot(*_high_level(bases, fac_ref, top, r))
        a = jnp.where(far, p_top, jnp.where(row >= col, p_f, p_b))
        a = a + jnp.where(row == col, p_b, 0.0)
        _finish_chunk(r, a, bases, v_ref, o_ref, bwd_refs, st_ref, fac_ref, tf_ref)


def _matmul_levels(v_ref, o_ref, bwd_refs, st_ref, stage):
    stq_ref, stk_ref, base_ref, fac_ref, tf_ref = stage
    row = lax.broadcasted_iota(jnp.int32, (CHUNK, CHUNK), 0)
    col = lax.broadcasted_iota(jnp.int32, (CHUNK, CHUNK), 1)
    xr = row ^ col
    lvl = jnp.zeros((CHUNK, CHUNK), jnp.int32)
    for b in range(N_LEVELS - 1):
        lvl = lvl + jnp.where(xr >= (1 << b), 1, 0)

    for r in range(CHUNKS_PER_BLOCK):
        bases = _chunk_bases(base_ref, r)
        a = None
        for li in range(N_LEVELS):
            if li < N_LOW:
                ql = _chunk_rows(stq_ref.at[li], r).astype(BF16)
                kl = _chunk_rows(stk_ref.at[li], r).astype(BF16)
            else:
                ql, kl = _high_level(bases, fac_ref, li - N_LOW, r)
            p = _nt_dot(ql, kl)
            a = p if a is None else jnp.where(lvl == li, p, a)
        _finish_chunk(r, a, bases, v_ref, o_ref, bwd_refs, st_ref, fac_ref, tf_ref)


def _rec_fwd_kernel(layer, lbf_ref, lbb_ref, q_ref, zf_ref, zb_ref, v_ref,
                    o_ref, qb_ref, kb_ref, vb_ref, tb_ref, *scratch):
    stage, st_ref = scratch[:-1], scratch[-1]
    bwd_refs = (qb_ref, kb_ref, vb_ref)

    @pl.when(pl.program_id(1) == 0)
    def _():
        st_ref[...] = jnp.zeros_like(st_ref)

    smallest = _scan_ratio(layer, lbf_ref, lbb_ref, q_ref, zf_ref, zb_ref, tb_ref, stage)
    safe = smallest >= RATIO_MIN

    @pl.when(safe)
    def _():
        _matmul_ratio(v_ref, o_ref, bwd_refs, st_ref, stage)

    @pl.when(jnp.logical_not(safe))
    def _():
        _scan_levels(layer, lbf_ref, lbb_ref, q_ref, zf_ref, zb_ref, tb_ref, stage)
        _matmul_levels(v_ref, o_ref, bwd_refs, st_ref, stage)


def _group_index(group):
    if F32_GROUP0 <= group < F32_GROUP0 + N_F32_GROUPS:
        return group - F32_GROUP0
    return group if group < F32_GROUP0 else group - N_F32_GROUPS


def _rec_fwd(proj_f, proj_h, lb_fwd, lb_bwd, layer, d_hgrn):
    s = proj_f.shape[1]
    depth = lb_fwd.shape[0]
    nh = d_hgrn // HEAD_DIM
    nblk = s // TOK_BLOCK
    blk = (TOK_BLOCK, HEAD_DIM)
    hblk = (None,) + blk
    here = lambda h, i: (h, i, 0)

    def col(group):
        return pl.BlockSpec(hblk, lambda h, i, g=_group_index(group): (g * nh + h, i, 0))

    lb_spec = pl.BlockSpec((depth, HEAD_DIM), lambda h, i: (0, h))
    return pl.pallas_call(
        functools.partial(_rec_fwd_kernel, layer),
        grid=(nh, nblk),
        in_specs=[lb_spec, lb_spec, col(GROUP_Q), col(GROUP_ZF), col(GROUP_ZB), col(GROUP_V)],
        out_specs=[
            pl.BlockSpec(hblk, here),
            pl.BlockSpec(hblk, here),
            pl.BlockSpec(hblk, here),
            pl.BlockSpec(hblk, here),
            pl.BlockSpec((None, CHUNKS_PER_BLOCK, HEAD_DIM), here),
        ],
        out_shape=[jax.ShapeDtypeStruct((nh, s, HEAD_DIM), F32),
                   jax.ShapeDtypeStruct((nh, s, HEAD_DIM), BF16),
                   jax.ShapeDtypeStruct((nh, s, HEAD_DIM), BF16),
                   jax.ShapeDtypeStruct((nh, s, HEAD_DIM), BF16),
                   jax.ShapeDtypeStruct((nh, nblk * CHUNKS_PER_BLOCK, HEAD_DIM), F32)],
        scratch_shapes=[
            pltpu.VMEM((N_LOW,) + blk, F32),
            pltpu.VMEM((N_LOW,) + blk, F32),
            pltpu.VMEM((N_BASE,) + blk, F32),
            pltpu.VMEM((FAC_ROWS, HEAD_DIM), F32),
            pltpu.VMEM((CHUNKS_PER_BLOCK, HEAD_DIM), F32),
            pltpu.VMEM((HEAD_DIM, HEAD_DIM), F32),
        ],
        compiler_params=_params(("parallel", "arbitrary")),
        name="rec_fwd",
    )(lb_fwd, lb_bwd, proj_h, proj_f, proj_f, proj_f)


def _rec_bwd_kernel(qb_ref, kb_ref, vb_ref, tb_ref, op_ref, g_ref, nw_ref, y_ref, ob_ref, st_ref):
    @pl.when(pl.program_id(1) == 0)
    def _():
        st_ref[...] = jnp.zeros_like(st_ref)

    for blk in reversed(range(BWD_BLOCKS)):
        for r in reversed(range(CHUNKS_PER_BLOCK)):
            st = st_ref[...]
            rows = pl.ds(blk * TOK_BLOCK + r * CHUNK, CHUNK)
            strided = pl.ds(blk * TOK_BLOCK + r, CHUNK, stride=CHUNKS_PER_BLOCK)
            ob_ref[strided, :] = _nt_dot(qb_ref[rows, :], st.astype(BF16))
            upd = lax.dot_general(vb_ref[rows, :], kb_ref[rows, :], _TN,
                                  preferred_element_type=F32)
            st_ref[...] = st * tb_ref[pl.ds(blk * CHUNKS_PER_BLOCK + r, 1), :] + upd
    g = g_ref[...].astype(F32)
    y = _rms(op_ref[...] + ob_ref[...], nw_ref[...]) * (g * _sigmoid(g))
    y_ref[...] = y.astype(y_ref.dtype)


def _rec_bwd(qb, kb, vb, tb, proj_h, o_part, norm_w, d_hgrn):
    s = proj_h.shape[1]
    nh = d_hgrn // HEAD_DIM
    rows = BWD_BLOCKS * TOK_BLOCK
    nstep = s // rows
    blk = (rows, HEAD_DIM)
    hblk = (None,) + blk
    rev = lambda h, i: (h, nstep - 1 - i, 0)

    def col(group):
        return pl.BlockSpec(hblk, lambda h, i, g=_group_index(group): (g * nh + h, nstep - 1 - i, 0))

    return pl.pallas_call(
        _rec_bwd_kernel,
        grid=(nh, nstep),
        in_specs=[
            pl.BlockSpec(hblk, rev),
            pl.BlockSpec(hblk, rev),
            pl.BlockSpec(hblk, rev),
            pl.BlockSpec((None, BWD_BLOCKS * CHUNKS_PER_BLOCK, HEAD_DIM), rev),
            pl.BlockSpec(hblk, rev),
            col(GROUP_GATE),
            pl.BlockSpec((1, HEAD_DIM), lambda h, i: (0, h)),
        ],
        out_specs=pl.BlockSpec(blk, lambda h, i: (nstep - 1 - i, h)),
        out_shape=jax.ShapeDtypeStruct((s, d_hgrn), BF16),
        scratch_shapes=[pltpu.VMEM(blk, F32), pltpu.VMEM((HEAD_DIM, HEAD_DIM), F32)],
        compiler_params=_params(("parallel", "arbitrary")),
        name="rec_bwd",
    )(qb, kb, vb, tb, o_part, proj_h, norm_w)


def _conv_kernel(b_ref, c_ref, h_ref, cp_ref, hp_ref, cn_ref, hn_ref, w_ref, y_ref):
    i = pl.program_id(0)
    last = pl.num_programs(0) - 1
    n = c_ref.shape[1]
    sub = lax.broadcasted_iota(jnp.int32, (SUBLANES, LANES), 0)
    for c in range(c_ref.shape[0]):
        u = c_ref[c].astype(F32) * h_ref[c].astype(F32)
        halo_p = (cp_ref[c].astype(F32) * hp_ref[c].astype(F32))[HALO_ROWS - SUBLANES:]
        halo_p = jnp.where(i > 0, halo_p, 0.0)
        first = jnp.where(sub == 0, pltpu.roll(halo_p, 1, 0), pltpu.roll(u[n - SUBLANES:], 1, 0))
        u_prev = jnp.concatenate([first, u[:n - SUBLANES]], axis=0)
        halo_n = (cn_ref[c].astype(F32) * hn_ref[c].astype(F32))[:SUBLANES]
        halo_n = jnp.where(i < last, halo_n, 0.0)
        lastv = jnp.where(sub == SUBLANES - 1, pltpu.roll(halo_n, SUBLANES - 1, 0),
                          pltpu.roll(u[:SUBLANES], SUBLANES - 1, 0))
        u_next = jnp.concatenate([u[SUBLANES:], lastv], axis=0)
        w = w_ref[:, c * LANES:(c + 1) * LANES]
        y = w[0:1] * u_prev + w[1:2] * u + w[2:3] * u_next
        y_ref[:, c * LANES:(c + 1) * LANES] = (b_ref[c].astype(F32) * y).astype(y_ref.dtype)


def _conv(proj_h, conv_w, d_hgrn, d_conv, *, tc):
    s = proj_h.shape[1]
    nblk = s // TOK_BLOCK
    nhalo = s // HALO_ROWS
    per_blk = TOK_BLOCK // HALO_ROWS
    slabs = tc // LANES
    ncol = d_conv // tc

    def first_block(group):
        return _group_index(group) * (d_hgrn // tc)

    def main(group):
        return pl.BlockSpec((slabs, TOK_BLOCK, LANES),
                            lambda i, j, b=first_block(group): (b + j, i, 0))

    def prev(group):
        return pl.BlockSpec((slabs, HALO_ROWS, LANES),
                            lambda i, j, b=first_block(group): (b + j,
                                                                jnp.maximum(i * per_blk - 1, 0), 0))

    def nxt(group):
        return pl.BlockSpec((slabs, HALO_ROWS, LANES),
                            lambda i, j, b=first_block(group): (
                                b + j, jnp.minimum((i + 1) * per_blk, nhalo - 1), 0))

    return pl.pallas_call(
        _conv_kernel,
        grid=(nblk, ncol),
        in_specs=[main(GROUP_CB), main(GROUP_CC), main(GROUP_CH), prev(GROUP_CC), prev(GROUP_CH),
                  nxt(GROUP_CC), nxt(GROUP_CH),
                  pl.BlockSpec((CONV_WIDTH, tc), lambda i, j: (0, j))],
        out_specs=pl.BlockSpec((TOK_BLOCK, tc), lambda i, j: (i, j)),
        out_shape=jax.ShapeDtypeStruct((s, d_conv), BF16),
        compiler_params=_params(("parallel", "parallel")),
        name="short_conv",
    )(proj_h, proj_h, proj_h, proj_h, proj_h, proj_h, proj_h, conv_w)


def _out_proj_kernel(x_ref, yr_ref, yc_ref, wr_ref, wc_ref, o_ref):
    acc = jnp.dot(yr_ref[...], wr_ref[...], preferred_element_type=F32)
    acc = acc + jnp.dot(yc_ref[...], wc_ref[...], preferred_element_type=F32)
    o_ref[...] = x_ref[...] + acc


def _out_proj(x, y_rec, y_conv, w, layer, *, tm):
    s, d = x.shape
    d_rec = y_rec.shape[1]
    d_conv = y_conv.shape[1]
    assert d_rec == d_conv
    return pl.pallas_call(
        _out_proj_kernel,
        grid=(s // tm,),
        in_specs=[
            pl.BlockSpec((tm, d), lambda i: (i, 0)),
            pl.BlockSpec((tm, d_rec), lambda i: (i, 0)),
            pl.BlockSpec((tm, d_conv), lambda i: (i, 0)),
            pl.BlockSpec((None, d_rec, d), lambda i: (layer, 0, 0)),
            pl.BlockSpec((None, d_conv, d), lambda i: (layer, 1, 0)),
        ],
        out_specs=pl.BlockSpec((tm, d), lambda i: (i, 0)),
        out_shape=jax.ShapeDtypeStruct((s, d), F32),
        compiler_params=_params(("parallel",)),
        name="out_proj",
    )(x, y_rec, y_conv, w, w)


def _ffn_kernel(final, x_ref, nw_ref, wg_ref, wu_ref, wd_ref, fw_ref, o_ref, h_scr):
    f = pl.program_id(1)

    @pl.when(f == 0)
    def _():
        x = x_ref[...]
        h_scr[...] = _rms(x, nw_ref[...]).astype(BF16)
        o_ref[...] = x

    h = h_scr[...]
    g = jnp.dot(h, wg_ref[...], preferred_element_type=F32)
    u = jnp.dot(h, wu_ref[...], preferred_element_type=F32)
    a = (g * _sigmoid(g) * u).astype(BF16)
    o_ref[...] += jnp.dot(a, wd_ref[...], preferred_element_type=F32)

    if final:
        @pl.when(f == pl.num_programs(1) - 1)
        def _():
            o_ref[...] = _rms(o_ref[...], fw_ref[...])


def _ffn(x, nw, w_gu, w_d, fw, layer, final, *, tm, tf):
    s, d = x.shape
    d_ff = w_d.shape[1]
    nf = d_ff // tf
    return pl.pallas_call(
        functools.partial(_ffn_kernel, final),
        grid=(s // tm, nf),
        in_specs=[
            pl.BlockSpec((tm, d), lambda i, f: (i, 0)),
            pl.BlockSpec((1, d), lambda i, f: (0, 0)),
            pl.BlockSpec((None, d, tf), lambda i, f: (layer, 0, f)),
            pl.BlockSpec((None, d, tf), lambda i, f: (layer, 0, nf + f)),
            pl.BlockSpec((None, tf, d), lambda i, f: (layer, f, 0)),
            pl.BlockSpec((1, d), lambda i, f: (0, 0)),
        ],
        out_specs=pl.BlockSpec((tm, d), lambda i, f: (i, 0)),
        out_shape=jax.ShapeDtypeStruct((s, d), F32),
        scratch_shapes=[pltpu.VMEM((tm, d), BF16)],
        compiler_params=_params(("parallel", "arbitrary")),
        name="ffn",
    )(x, nw, w_gu, w_gu, w_d, fw)


def _interleave(x):
    s, d = x.shape
    x4 = x.reshape(s // TOK_BLOCK, CHUNKS_PER_BLOCK, CHUNK, d)
    return x4.transpose(0, 2, 1, 3).reshape(s, d)


def _deinterleave(x):
    s, d = x.shape
    x4 = x.reshape(s // TOK_BLOCK, CHUNK, CHUNKS_PER_BLOCK, d)
    return x4.transpose(0, 2, 1, 3).reshape(s, d)


def kernel(x, attn_norm_w, w_in, lb_fwd, lb_bwd, hgrn_norm_w, conv_w, w_out, ffn_norm_w,
           w_gate_up, w_down, final_norm_w):
    bsz, s, d = x.shape
    depth = w_in.shape[0]
    d_hgrn = lb_fwd.shape[1]
    d_conv = conv_w.shape[2]
    assert bsz == 1 and s % (BWD_BLOCKS * TOK_BLOCK) == 0
    assert w_in.shape[2] == 5 * d_hgrn + 3 * d_conv and d_hgrn == d_conv

    xs = _interleave(x[0])
    w_in, w_out, w_gate_up, w_down = (_cast_bf16(w) for w in (w_in, w_out, w_gate_up, w_down))
    fw = final_norm_w.reshape(1, d)
    for l in range(depth):
        proj_f, proj_h = _in_proj(xs, attn_norm_w[l].reshape(1, d), w_in, l, tm=1024, tn=d_hgrn)
        o_part, qb, kb, vb, tb = _rec_fwd(proj_f, proj_h, lb_fwd, lb_bwd, l, d_hgrn)
        y_rec = _rec_bwd(qb, kb, vb, tb, proj_h, o_part, hgrn_norm_w[l].reshape(1, d_hgrn), d_hgrn)
        y_conv = _conv(proj_h, conv_w[l], d_hgrn, d_conv, tc=512)
        xs = _out_proj(xs, y_rec, y_conv, w_out, l, tm=512)
        xs = _ffn(xs, ffn_norm_w[l].reshape(1, d), w_gate_up, w_down, fw, l, l == depth - 1,
                  tm=512, tf=512)
    return _deinterleave(xs)[None]
```

```python
import functools

import jax
import jax.numpy as jnp
from jax import lax
from jax.experimental import pallas as pl
from jax.experimental.pallas import tpu as pltpu

F32 = jnp.float32
BF16 = jnp.bfloat16

EPS = 1e-6
HEAD_DIM = 128
CHUNK = 128
SUBLANES = 8
LANES = 128
CHUNKS_PER_BLOCK = SUBLANES
TOK_BLOCK = CHUNK * CHUNKS_PER_BLOCK
POS_BLOCK = SUBLANES
N_GROUPS = CHUNK // POS_BLOCK
GROUP_ROWS = POS_BLOCK * CHUNKS_PER_BLOCK
LOW_LEVELS = tuple(1 << b for b in range(POS_BLOCK.bit_length() - 1))
HIGH_SPANS = tuple(1 << b for b in range(N_GROUPS.bit_length() - 1))
N_LOW = len(LOW_LEVELS) + 1
N_LEVELS = N_LOW + len(HIGH_SPANS)
N_FAC_LEVEL = 2 * (len(HIGH_SPANS) - 1)
FAC_QF, FAC_KF, FAC_QB, FAC_KB = (N_FAC_LEVEL + i for i in range(4))
FAC_RQF, FAC_RKF, FAC_RQB, FAC_RKB = (N_FAC_LEVEL + 4 + i for i in range(4))
N_FAC = N_FAC_LEVEL + 8
FAC_ROWS = N_FAC * N_GROUPS * CHUNKS_PER_BLOCK
BASE_QP, BASE_KS, BASE_QS, BASE_KP, BASE_QI, BASE_KI = range(6)
N_BASE = 6
RATIO_SPAN = 4
RATIO_MIN = 2.0 ** -100
BWD_BLOCKS = 4
GROUP_Q, GROUP_ZF, GROUP_ZB, GROUP_V, GROUP_GATE, GROUP_CB, GROUP_CC, GROUP_CH = range(8)
F32_GROUP0, N_F32_GROUPS = GROUP_ZF, 3
CONV_WIDTH = 3
HALO_ROWS = 2 * SUBLANES
VMEM_LIMIT = 52 * 1024 * 1024
CAST_BLOCK_BYTES = 8 * 1024 * 1024

_NT = (((1,), (1,)), ((), ()))
_TN = (((0,), (0,)), ((), ()))


def _sigmoid(x):
    return 0.5 * jnp.tanh(0.5 * x) + 0.5


def _rms(x, w):
    ms = jnp.mean(x * x, axis=-1, keepdims=True)
    return x * lax.rsqrt(ms + EPS) * w


def _params(sem):
    return pltpu.CompilerParams(dimension_semantics=sem, vmem_limit_bytes=VMEM_LIMIT)


def _cast_kernel(w_ref, o_ref):
    o_ref[...] = w_ref[...].astype(o_ref.dtype)


def _cast_bf16(w):
    depth, r, c = w.shape
    rows = depth * r
    tr = 1 << ((CAST_BLOCK_BYTES // (4 * c)).bit_length() - 1)
    assert tr >= 2 * SUBLANES and rows % tr == 0
    out = pl.pallas_call(
        _cast_kernel,
        grid=(rows // tr,),
        in_specs=[pl.BlockSpec((tr, c), lambda i: (i, 0))],
        out_specs=pl.BlockSpec((tr, c), lambda i: (i, 0)),
        out_shape=jax.ShapeDtypeStruct((rows, c), BF16),
        compiler_params=_params(("parallel",)),
        name="cast_bf16",
    )(w.reshape(rows, c))
    return out.reshape(depth, r, c)


def _in_proj_kernel(x_ref, nw_ref, w_ref, of_ref, oh_ref, h_scr):
    j = pl.program_id(1)

    @pl.when(j == 0)
    def _():
        h_scr[...] = _rms(x_ref[...], nw_ref[...]).astype(BF16)

    def project(o_ref):
        res = jnp.dot(h_scr[...], w_ref[...], preferred_element_type=F32)
        for c in range(o_ref.shape[0]):
            o_ref[c] = res[:, c * LANES:(c + 1) * LANES].astype(o_ref.dtype)

    wide = jnp.logical_and(j >= F32_GROUP0, j < F32_GROUP0 + N_F32_GROUPS)
    pl.when(wide)(functools.partial(project, of_ref))
    pl.when(jnp.logical_not(wide))(functools.partial(project, oh_ref))


def _in_proj(x, nw, w, layer, *, tm, tn):
    s, d = x.shape
    n = w.shape[2]
    ngroups = n // tn
    slabs = tn // LANES
    assert F32_GROUP0 == 1 and ngroups > F32_GROUP0 + N_F32_GROUPS
    nhalf = ngroups - N_F32_GROUPS
    return pl.pallas_call(
        _in_proj_kernel,
        grid=(s // tm, ngroups),
        in_specs=[
            pl.BlockSpec((tm, d), lambda i, j: (i, 0)),
            pl.BlockSpec((1, d), lambda i, j: (0, 0)),
            pl.BlockSpec((None, d, tn), lambda i, j: (layer, 0, j)),
        ],
        out_specs=[
            pl.BlockSpec((slabs, tm, LANES),
                         lambda i, j: (jnp.clip(j - F32_GROUP0, 0, N_F32_GROUPS - 1), i, 0)),
            pl.BlockSpec((slabs, tm, LANES),
                         lambda i, j: (jnp.clip(j - N_F32_GROUPS, 0, nhalf - 1), i, 0)),
        ],
        out_shape=[jax.ShapeDtypeStruct((N_F32_GROUPS * slabs, s, LANES), F32),
                   jax.ShapeDtypeStruct((nhalf * slabs, s, LANES), BF16)],
        scratch_shapes=[pltpu.VMEM((tm, d), BF16)],
        compiler_params=_params(("parallel", "arbitrary")),
        name="in_proj",
    )(x, nw, w)


def _lower_bound(lb_ref, layer):
    p = lb_ref[...]
    if layer == 0:
        return jnp.zeros((1, p.shape[1]), F32)
    e = jnp.exp(p - jnp.max(p, axis=0, keepdims=True))
    sm = e / jnp.sum(e, axis=0, keepdims=True)
    return jnp.sum(sm[1:layer + 1], axis=0, keepdims=True)


def _gates(z, lb):
    th = 0.5 * jnp.tanh(0.5 * z)
    half = 0.5 * (1.0 - lb)
    f = (lb + half) + (1.0 - lb) * th
    k = half - (1.0 - lb) * th
    return f, k


def _halves(x, h):
    x5 = x.reshape((x.shape[0] // (2 * h), 2, h) + x.shape[1:])
    return x5[:, 0], x5[:, 1]


def _join(a, b):
    x = jnp.stack([a, b], axis=1)
    return x.reshape((x.shape[0] * x.shape[1] * x.shape[2],) + a.shape[2:])


def _chunk_rows(ref, r):
    return ref[pl.ds(r, CHUNK, stride=CHUNKS_PER_BLOCK), :]


def _fac_rows(k, g):
    return pl.ds((k * N_GROUPS + g) * CHUNKS_PER_BLOCK, CHUNKS_PER_BLOCK)


def _group_factor(fac_ref, k, r):
    return jnp.concatenate(
        [jnp.broadcast_to(fac_ref[pl.ds((k * N_GROUPS + g) * CHUNKS_PER_BLOCK + r, 1), :],
                          (POS_BLOCK, HEAD_DIM))
         for g in range(N_GROUPS)], axis=0)


def _by_group_parity(even, odd, span):
    pieces = []
    for g in range(N_GROUPS):
        src = odd if (g // span) % 2 else even
        pieces.append(src[g * POS_BLOCK:(g + 1) * POS_BLOCK])
    return jnp.concatenate(pieces, axis=0)


def _seg_products(tot, span):
    ones = jnp.ones_like(tot[0])
    pre, suf = [None] * N_GROUPS, [None] * N_GROUPS
    for s0 in range(0, N_GROUPS, span):
        acc = None
        for g in range(s0, s0 + span):
            pre[g] = ones if acc is None else acc
            acc = tot[g] if acc is None else acc * tot[g]
        acc = None
        for g in reversed(range(s0, s0 + span)):
            suf[g] = ones if acc is None else acc
            acc = tot[g] if acc is None else acc * tot[g]
    return pre, suf


def _group_scan(g, q_ref, zf_ref, zb_ref, lbf, lbb, low_stage=None):
    sh = (POS_BLOCK, CHUNKS_PER_BLOCK, HEAD_DIM)
    flat = (GROUP_ROWS, HEAD_DIM)
    rows = pl.ds(g * GROUP_ROWS, GROUP_ROWS)
    q = q_ref[rows, :].astype(F32).reshape(sh)
    qs = q * _sigmoid(q)
    f_f, k_f = _gates(zf_ref[rows, :].reshape(sh), lbf)
    f_b, k_b = _gates(zb_ref[rows, :].reshape(sh), lbb)
    if low_stage is not None:
        stq_ref, stk_ref = low_stage
        stq_ref[0, rows, :] = qs.reshape(flat)
        stk_ref[0, rows, :] = (k_f + k_b).reshape(flat)
    pin, sex, tf = f_f, jnp.ones(sh, F32), f_f
    sin, pex, tb = f_b, jnp.ones(sh, F32), f_b
    for li, h in enumerate(LOW_LEVELS):
        pin1, pin2 = _halves(pin, h)
        sex1, sex2 = _halves(sex, h)
        sin1, sin2 = _halves(sin, h)
        pex1, pex2 = _halves(pex, h)
        if low_stage is not None:
            qs1, qs2 = _halves(qs, h)
            kf1, _ = _halves(k_f, h)
            _, kb2 = _halves(k_b, h)
            stq_ref[li + 1, rows, :] = _join(qs1 * sin1, qs2 * pin2).reshape(flat)
            stk_ref[li + 1, rows, :] = _join(kf1 * sex1, kb2 * pex2).reshape(flat)
        tf1, tf2 = _halves(tf, 1)
        tb1, tb2 = _halves(tb, 1)
        pin = _join(pin1, pin2 * tf1)
        sex = _join(sex1 * tf2, sex2)
        sin = _join(sin1 * tb2, sin2)
        pex = _join(pex1, pex2 * tb1)
        tf = (tf1 * tf2).reshape((tf.shape[0] // 2,) + sh[1:])
        tb = (tb1 * tb2).reshape((tb.shape[0] // 2,) + sh[1:])
    return rows, qs, k_f, k_b, pin, sex, sin, pex, tf[0], tb[0]


def _chunk_factors(tot_f, tot_b, fac_ref, tf_ref, tb_ref):
    pre_f, suf_f = _seg_products(tot_f, N_GROUPS)
    pre_b, suf_b = _seg_products(tot_b, N_GROUPS)
    for g in range(N_GROUPS):
        fac_ref[_fac_rows(FAC_QF, g), :] = pre_f[g]
        fac_ref[_fac_rows(FAC_KF, g), :] = suf_f[g]
        fac_ref[_fac_rows(FAC_QB, g), :] = suf_b[g]
        fac_ref[_fac_rows(FAC_KB, g), :] = pre_b[g]
    tf_ref[...] = pre_f[N_GROUPS - 1] * tot_f[N_GROUPS - 1]
    tb_ref[...] = pre_b[N_GROUPS - 1] * tot_b[N_GROUPS - 1]


def _level_factors(tot_f, tot_b, fac_ref, si):
    span = HIGH_SPANS[si]
    pre_f, suf_f = _seg_products(tot_f, span)
    pre_b, suf_b = _seg_products(tot_b, span)
    for g in range(N_GROUPS):
        second = (g // span) % 2
        fac_ref[_fac_rows(2 * (si - 1), g), :] = pre_f[g] if second else suf_b[g]
        fac_ref[_fac_rows(2 * (si - 1) + 1, g), :] = pre_b[g] if second else suf_f[g]


def _scan_ratio(layer, lbf_ref, lbb_ref, q_ref, zf_ref, zb_ref, tb_ref, stage):
    _, _, base_ref, fac_ref, tf_ref = stage
    lbf = _lower_bound(lbf_ref, layer)
    lbb = _lower_bound(lbb_ref, layer)
    flat = (GROUP_ROWS, HEAD_DIM)
    tot_f, tot_b = [], []
    for g in range(N_GROUPS):
        rows, qs, k_f, k_b, pin, sex, sin, pex, tf, tb = _group_scan(
            g, q_ref, zf_ref, zb_ref, lbf, lbb)
        base_ref[BASE_QP, rows, :] = (qs * pin).reshape(flat)
        base_ref[BASE_KS, rows, :] = (k_f * sex).reshape(flat)
        base_ref[BASE_QS, rows, :] = (qs * sin).reshape(flat)
        base_ref[BASE_KP, rows, :] = (k_b * pex).reshape(flat)
        if (g // RATIO_SPAN) % 2:
            base_ref[BASE_QI, rows, :] = (qs * (1.0 / pex)).reshape(flat)
            base_ref[BASE_KI, rows, :] = (k_f * (1.0 / pin)).reshape(flat)
        else:
            base_ref[BASE_QI, rows, :] = (qs * (1.0 / sex)).reshape(flat)
            base_ref[BASE_KI, rows, :] = (k_b * (1.0 / sin)).reshape(flat)
        tot_f.append(tf)
        tot_b.append(tb)

    pre_f, suf_f = _seg_products(tot_f, RATIO_SPAN)
    pre_b, suf_b = _seg_products(tot_b, RATIO_SPAN)
    smallest = None
    for g in range(N_GROUPS):
        second = (g // RATIO_SPAN) % 2
        fac_ref[_fac_rows(FAC_RQF, g), :] = pre_f[g] if second else 1.0 / suf_f[g]
        fac_ref[_fac_rows(FAC_RKF, g), :] = 1.0 / pre_f[g] if second else suf_f[g]
        fac_ref[_fac_rows(FAC_RQB, g), :] = 1.0 / pre_b[g] if second else suf_b[g]
        fac_ref[_fac_rows(FAC_RKB, g), :] = pre_b[g] if second else 1.0 / suf_b[g]
        if g % RATIO_SPAN == RATIO_SPAN - 1:
            run = jnp.minimum(pre_f[g] * tot_f[g], pre_b[g] * tot_b[g])
            smallest = run if smallest is None else jnp.minimum(smallest, run)
    _level_factors(tot_f, tot_b, fac_ref, len(HIGH_SPANS) - 1)
    _chunk_factors(tot_f, tot_b, fac_ref, tf_ref, tb_ref)
    return jnp.min(smallest)


def _scan_levels(layer, lbf_ref, lbb_ref, q_ref, zf_ref, zb_ref, tb_ref, stage):
    stq_ref, stk_ref, base_ref, fac_ref, tf_ref = stage
    lbf = _lower_bound(lbf_ref, layer)
    lbb = _lower_bound(lbb_ref, layer)
    flat = (GROUP_ROWS, HEAD_DIM)
    tot_f, tot_b = [], []
    for g in range(N_GROUPS):
        rows, qs, k_f, k_b, pin, sex, sin, pex, tf, tb = _group_scan(
            g, q_ref, zf_ref, zb_ref, lbf, lbb, (stq_ref, stk_ref))
        base_ref[BASE_QP, rows, :] = (qs * pin).reshape(flat)
        base_ref[BASE_KS, rows, :] = (k_f * sex).reshape(flat)
        base_ref[BASE_QS, rows, :] = (qs * sin).reshape(flat)
        base_ref[BASE_KP, rows, :] = (k_b * pex).reshape(flat)
        tot_f.append(tf)
        tot_b.append(tb)
    for si in range(1, len(HIGH_SPANS)):
        _level_factors(tot_f, tot_b, fac_ref, si)
    _chunk_factors(tot_f, tot_b, fac_ref, tf_ref, tb_ref)


def _chunk_bases(base_ref, r):
    return tuple(_chunk_rows(base_ref.at[k], r) for k in (BASE_QP, BASE_KS, BASE_QS, BASE_KP))


def _high_level(bases, fac_ref, si, r):
    qp, ks, qs, kp = bases
    span = HIGH_SPANS[si]
    ql = _by_group_parity(qs, qp, span)
    kl = _by_group_parity(ks, kp, span)
    if si > 0:
        ql = ql * _group_factor(fac_ref, 2 * (si - 1), r)
        kl = kl * _group_factor(fac_ref, 2 * (si - 1) + 1, r)
    return ql.astype(BF16), kl.astype(BF16)


def _nt_dot(a, b):
    return lax.dot_general(a, b, _NT, preferred_element_type=F32)


def _finish_chunk(r, a, bases, v_ref, o_ref, bwd_refs, st_ref, fac_ref, tf_ref):
    qp, ks, qs, kp = bases
    vb = _chunk_rows(v_ref, r).astype(BF16)
    st = st_ref[...]
    o = jnp.dot(a.astype(BF16), vb, preferred_element_type=F32)
    qf = (qp * _group_factor(fac_ref, FAC_QF, r)).astype(BF16)
    o = o + _nt_dot(qf, st.astype(BF16))
    kf = (ks * _group_factor(fac_ref, FAC_KF, r)).astype(BF16)
    upd = lax.dot_general(vb, kf, _TN, preferred_element_type=F32)
    st_ref[...] = st * tf_ref[pl.ds(r, 1), :] + upd
    o_ref[pl.ds(r, CHUNK, stride=CHUNKS_PER_BLOCK), :] = o
    rows = pl.ds(r * CHUNK, CHUNK)
    qb_ref, kb_ref, vb_ref = bwd_refs
    qb_ref[rows, :] = (qs * _group_factor(fac_ref, FAC_QB, r)).astype(BF16)
    kb_ref[rows, :] = (kp * _group_factor(fac_ref, FAC_KB, r)).astype(BF16)
    vb_ref[rows, :] = vb


def _matmul_ratio(v_ref, o_ref, bwd_refs, st_ref, stage):
    _, _, base_ref, fac_ref, tf_ref = stage
    row = lax.broadcasted_iota(jnp.int32, (CHUNK, CHUNK), 0)
    col = lax.broadcasted_iota(jnp.int32, (CHUNK, CHUNK), 1)
    far = (row ^ col) >= 2 * RATIO_SPAN * POS_BLOCK
    top = len(HIGH_SPANS) - 1
    for r in range(CHUNKS_PER_BLOCK):
        bases = _chunk_bases(base_ref, r)
        qp, ks, qs, kp = bases
        qi = _chunk_rows(base_ref.at[BASE_QI], r)
        ki = _chunk_rows(base_ref.at[BASE_KI], r)
        qf = _by_group_parity(qi, qp, RATIO_SPAN) * _group_factor(fac_ref, FAC_RQF, r)
        kf = _by_group_parity(ks, ki, RATIO_SPAN) * _group_factor(fac_ref, FAC_RKF, r)
        qb = _by_group_parity(qs, qi, RATIO_SPAN) * _group_factor(fac_ref, FAC_RQB, r)
        kb = _by_group_parity(ki, kp, RATIO_SPAN) * _group_factor(fac_ref, FAC_RKB, r)
        p_f = _nt_dot(qf.astype(BF16), kf.astype(BF16))
        p_b = _nt_dot(qb.astype(BF16), kb.astype(BF16))
        p_top = _nt_dot(*_high_level(bases, fac_ref, top, r))
        a = jnp.where(far, p_top, jnp.where(row >= col, p_f, p_b))
        a = a + jnp.where(row == col, p_b, 0.0)
        _finish_chunk(r, a, bases, v_ref, o_ref, bwd_refs, st_ref, fac_ref, tf_ref)


def _matmul_levels(v_ref, o_ref, bwd_refs, st_ref, stage):
    stq_ref, stk_ref, base_ref, fac_ref, tf_ref = stage
    row = lax.broadcasted_iota(jnp.int32, (CHUNK, CHUNK), 0)
    col = lax.broadcasted_iota(jnp.int32, (CHUNK, CHUNK), 1)
    xr = row ^ col
    lvl = jnp.zeros((CHUNK, CHUNK), jnp.int32)
    for b in range(N_LEVELS - 1):
        lvl = lvl + jnp.where(xr >= (1 << b), 1, 0)

    for r in range(CHUNKS_PER_BLOCK):
        bases = _chunk_bases(base_ref, r)
        a = None
        for li in range(N_LEVELS):
            if li < N_LOW:
                ql = _chunk_rows(stq_ref.at[li], r).astype(BF16)
                kl = _chunk_rows(stk_ref.at[li], r).astype(BF16)
            else:
                ql, kl = _high_level(bases, fac_ref, li - N_LOW, r)
            p = _nt_dot(ql, kl)
            a = p if a is None else jnp.where(lvl == li, p, a)
        _finish_chunk(r, a, bases, v_ref, o_ref, bwd_refs, st_ref, fac_ref, tf_ref)


def _rec_fwd_kernel(layer, lbf_ref, lbb_ref, q_ref, zf_ref, zb_ref, v_ref,
                    o_ref, qb_ref, kb_ref, vb_ref, tb_ref, *scratch):
    stage, st_ref = scratch[:-1], scratch[-1]
    bwd_refs = (qb_ref, kb_ref, vb_ref)

    @pl.when(pl.program_id(1) == 0)
    def _():
        st_ref[...] = jnp.zeros_like(st_ref)

    smallest = _scan_ratio(layer, lbf_ref, lbb_ref, q_ref, zf_ref, zb_ref, tb_ref, stage)
    safe = smallest >= RATIO_MIN

    @pl.when(safe)
    def _():
        _matmul_ratio(v_ref, o_ref, bwd_refs, st_ref, stage)

    @pl.when(jnp.logical_not(safe))
    def _():
        _scan_levels(layer, lbf_ref, lbb_ref, q_ref, zf_ref, zb_ref, tb_ref, stage)
        _matmul_levels(v_ref, o_ref, bwd_refs, st_ref, stage)


def _group_index(group):
    if F32_GROUP0 <= group < F32_GROUP0 + N_F32_GROUPS:
        return group - F32_GROUP0
    return group if group < F32_GROUP0 else group - N_F32_GROUPS


def _rec_fwd(proj_f, proj_h, lb_fwd, lb_bwd, layer, d_hgrn):
    s = proj_f.shape[1]
    depth = lb_fwd.shape[0]
    nh = d_hgrn // HEAD_DIM
    nblk = s // TOK_BLOCK
    blk = (TOK_BLOCK, HEAD_DIM)
    hblk = (None,) + blk
    here = lambda h, i: (h, i, 0)

    def col(group):
        return pl.BlockSpec(hblk, lambda h, i, g=_group_index(group): (g * nh + h, i, 0))

    lb_spec = pl.BlockSpec((depth, HEAD_DIM), lambda h, i: (0, h))
    return pl.pallas_call(
        functools.partial(_rec_fwd_kernel, layer),
        grid=(nh, nblk),
        in_specs=[lb_spec, lb_spec, col(GROUP_Q), col(GROUP_ZF), col(GROUP_ZB), col(GROUP_V)],
        out_specs=[
            pl.BlockSpec(hblk, here),
            pl.BlockSpec(hblk, here),
            pl.BlockSpec(hblk, here),
            pl.BlockSpec(hblk, here),
            pl.BlockSpec((None, CHUNKS_PER_BLOCK, HEAD_DIM), here),
        ],
        out_shape=[jax.ShapeDtypeStruct((nh, s, HEAD_DIM), F32),
                   jax.ShapeDtypeStruct((nh, s, HEAD_DIM), BF16),
                   jax.ShapeDtypeStruct((nh, s, HEAD_DIM), BF16),
                   jax.ShapeDtypeStruct((nh, s, HEAD_DIM), BF16),
                   jax.ShapeDtypeStruct((nh, nblk * CHUNKS_PER_BLOCK, HEAD_DIM), F32)],
        scratch_shapes=[
            pltpu.VMEM((N_LOW,) + blk, F32),
            pltpu.VMEM((N_LOW,) + blk, F32),
            pltpu.VMEM((N_BASE,) + blk, F32),
            pltpu.VMEM((FAC_ROWS, HEAD_DIM), F32),
            pltpu.VMEM((CHUNKS_PER_BLOCK, HEAD_DIM), F32),
            pltpu.VMEM((HEAD_DIM, HEAD_DIM), F32),
        ],
        compiler_params=_params(("parallel", "arbitrary")),
        name="rec_fwd",
    )(lb_fwd, lb_bwd, proj_h, proj_f, proj_f, proj_f)


def _rec_bwd_kernel(qb_ref, kb_ref, vb_ref, tb_ref, op_ref, g_ref, nw_ref, y_ref, ob_ref, st_ref):
    @pl.when(pl.program_id(1) == 0)
    def _():
        st_ref[...] = jnp.zeros_like(st_ref)

    for blk in reversed(range(BWD_BLOCKS)):
        for r in reversed(range(CHUNKS_PER_BLOCK)):
            st = st_ref[...]
            rows = pl.ds(blk * TOK_BLOCK + r * CHUNK, CHUNK)
            strided = pl.ds(blk * TOK_BLOCK + r, CHUNK, stride=CHUNKS_PER_BLOCK)
            ob_ref[strided, :] = _nt_dot(qb_ref[rows, :], st.astype(BF16))
            upd = lax.dot_general(vb_ref[rows, :], kb_ref[rows, :], _TN,
                                  preferred_element_type=F32)
            st_ref[...] = st * tb_ref[pl.ds(blk * CHUNKS_PER_BLOCK + r, 1), :] + upd
    g = g_ref[...].astype(F32)
    y = _rms(op_ref[...] + ob_ref[...], nw_ref[...]) * (g * _sigmoid(g))
    y_ref[...] = y.astype(y_ref.dtype)


def _rec_bwd(qb, kb, vb, tb, proj_h, o_part, norm_w, d_hgrn):
    s = proj_h.shape[1]
    nh = d_hgrn // HEAD_DIM
    rows = BWD_BLOCKS * TOK_BLOCK
    nstep = s // rows
    blk = (rows, HEAD_DIM)
    hblk = (None,) + blk
    rev = lambda h, i: (h, nstep - 1 - i, 0)

    def col(group):
        return pl.BlockSpec(hblk, lambda h, i, g=_group_index(group): (g * nh + h, nstep - 1 - i, 0))

    return pl.pallas_call(
        _rec_bwd_kernel,
        grid=(nh, nstep),
        in_specs=[
            pl.BlockSpec(hblk, rev),
            pl.BlockSpec(hblk, rev),
            pl.BlockSpec(hblk, rev),
            pl.BlockSpec((None, BWD_BLOCKS * CHUNKS_PER_BLOCK, HEAD_DIM), rev),
            pl.BlockSpec(hblk, rev),
            col(GROUP_GATE),
            pl.BlockSpec((1, HEAD_DIM), lambda h, i: (0, h)),
        ],
        out_specs=pl.BlockSpec(blk, lambda h, i: (nstep - 1 - i, h)),
        out_shape=jax.ShapeDtypeStruct((s, d_hgrn), BF16),
        scratch_shapes=[pltpu.VMEM(blk, F32), pltpu.VMEM((HEAD_DIM, HEAD_DIM), F32)],
        compiler_params=_params(("parallel", "arbitrary")),
        name="rec_bwd",
    )(qb, kb, vb, tb, o_part, proj_h, norm_w)


def _conv_kernel(b_ref, c_ref, h_ref, cp_ref, hp_ref, cn_ref, hn_ref, w_ref, y_ref):
    i = pl.program_id(0)
    last = pl.num_programs(0) - 1
    n = c_ref.shape[1]
    sub = lax.broadcasted_iota(jnp.int32, (SUBLANES, LANES), 0)
    for c in range(c_ref.shape[0]):
        u = c_ref[c].astype(F32) * h_ref[c].astype(F32)
        halo_p = (cp_ref[c].astype(F32) * hp_ref[c].astype(F32))[HALO_ROWS - SUBLANES:]
        halo_p = jnp.where(i > 0, halo_p, 0.0)
        first = jnp.where(sub == 0, pltpu.roll(halo_p, 1, 0), pltpu.roll(u[n - SUBLANES:], 1, 0))
        u_prev = jnp.concatenate([first, u[:n - SUBLANES]], axis=0)
        halo_n = (cn_ref[c].astype(F32) * hn_ref[c].astype(F32))[:SUBLANES]
        halo_n = jnp.where(i < last, halo_n, 0.0)
        lastv = jnp.where(sub == SUBLANES - 1, pltpu.roll(halo_n, SUBLANES - 1, 0),
                          pltpu.roll(u[:SUBLANES], SUBLANES - 1, 0))
        u_next = jnp.concatenate([u[SUBLANES:], lastv], axis=0)
        w = w_ref[:, c * LANES:(c + 1) * LANES]
        y = w[0:1] * u_prev + w[1:2] * u + w[2:3] * u_next
        y_ref[:, c * LANES:(c + 1) * LANES] = (b_ref[c].astype(F32) * y).astype(y_ref.dtype)


def _conv(proj_h, conv_w, d_hgrn, d_conv, *, tc):
    s = proj_h.shape[1]
    nblk = s // TOK_BLOCK
    nhalo = s // HALO_ROWS
    per_blk = TOK_BLOCK // HALO_ROWS
    slabs = tc // LANES
    ncol = d_conv // tc

    def first_block(group):
        return _group_index(group) * (d_hgrn // tc)

    def main(group):
        return pl.BlockSpec((slabs, TOK_BLOCK, LANES),
                            lambda i, j, b=first_block(group): (b + j, i, 0))

    def prev(group):
        return pl.BlockSpec((slabs, HALO_ROWS, LANES),
                            lambda i, j, b=first_block(group): (b + j,
                                                                jnp.maximum(i * per_blk - 1, 0), 0))

    def nxt(group):
        return pl.BlockSpec((slabs, HALO_ROWS, LANES),
                            lambda i, j, b=first_block(group): (
                                b + j, jnp.minimum((i + 1) * per_blk, nhalo - 1), 0))

    return pl.pallas_call(
        _conv_kernel,
        grid=(nblk, ncol),
        in_specs=[main(GROUP_CB), main(GROUP_CC), main(GROUP_CH), prev(GROUP_CC), prev(GROUP_CH),
                  nxt(GROUP_CC), nxt(GROUP_CH),
                  pl.BlockSpec((CONV_WIDTH, tc), lambda i, j: (0, j))],
        out_specs=pl.BlockSpec((TOK_BLOCK, tc), lambda i, j: (i, j)),
        out_shape=jax.ShapeDtypeStruct((s, d_conv), BF16),
        compiler_params=_params(("parallel", "parallel")),
        name="short_conv",
    )(proj_h, proj_h, proj_h, proj_h, proj_h, proj_h, proj_h, conv_w)


def _out_proj_kernel(x_ref, yr_ref, yc_ref, wr_ref, wc_ref, o_ref):
    acc = jnp.dot(yr_ref[...], wr_ref[...], preferred_element_type=F32)
    acc = acc + jnp.dot(yc_ref[...], wc_ref[...], preferred_element_type=F32)
    o_ref[...] = x_ref[...] + acc


def _out_proj(x, y_rec, y_conv, w, layer, *, tm):
    s, d = x.shape
    d_rec = y_rec.shape[1]
    d_conv = y_conv.shape[1]
    assert d_rec == d_conv
    return pl.pallas_call(
        _out_proj_kernel,
        grid=(s // tm,),
        in_specs=[
            pl.BlockSpec((tm, d), lambda i: (i, 0)),
            pl.BlockSpec((tm, d_rec), lambda i: (i, 0)),
            pl.BlockSpec((tm, d_conv), lambda i: (i, 0)),
            pl.BlockSpec((None, d_rec, d), lambda i: (layer, 0, 0)),
            pl.BlockSpec((None, d_conv, d), lambda i: (layer, 1, 0)),
        ],
        out_specs=pl.BlockSpec((tm, d), lambda i: (i, 0)),
        out_shape=jax.ShapeDtypeStruct((s, d), F32),
        compiler_params=_params(("parallel",)),
        name="out_proj",
    )(x, y_rec, y_conv, w, w)


def _ffn_kernel(final, x_ref, nw_ref, wg_ref, wu_ref, wd_ref, fw_ref, o_ref, h_scr):
    f = pl.program_id(1)

    @pl.when(f == 0)
    def _():
        x = x_ref[...]
        h_scr[...] = _rms(x, nw_ref[...]).astype(BF16)
        o_ref[...] = x

    h = h_scr[...]
    g = jnp.dot(h, wg_ref[...], preferred_element_type=F32)
    u = jnp.dot(h, wu_ref[...], preferred_element_type=F32)
    a = (g * _sigmoid(g) * u).astype(BF16)
    o_ref[...] += jnp.dot(a, wd_ref[...], preferred_element_type=F32)

    if final:
        @pl.when(f == pl.num_programs(1) - 1)
        def _():
            o_ref[...] = _rms(o_ref[...], fw_ref[...])


def _ffn(x, nw, w_gu, w_d, fw, layer, final, *, tm, tf):
    s, d = x.shape
    d_ff = w_d.shape[1]
    nf = d_ff // tf
    return pl.pallas_call(
        functools.partial(_ffn_kernel, final),
        grid=(s // tm, nf),
        in_specs=[
            pl.BlockSpec((tm, d), lambda i, f: (i, 0)),
            pl.BlockSpec((1, d), lambda i, f: (0, 0)),
            pl.BlockSpec((None, d, tf), lambda i, f: (layer, 0, f)),
            pl.BlockSpec((None, d, tf), lambda i, f: (layer, 0, nf + f)),
            pl.BlockSpec((None, tf, d), lambda i, f: (layer, f, 0)),
            pl.BlockSpec((1, d), lambda i, f: (0, 0)),
        ],
        out_specs=pl.BlockSpec((tm, d), lambda i, f: (i, 0)),
        out_shape=jax.ShapeDtypeStruct((s, d), F32),
        scratch_shapes=[pltpu.VMEM((tm, d), BF16)],
        compiler_params=_params(("parallel", "arbitrary")),
        name="ffn",
    )(x, nw, w_gu, w_gu, w_d, fw)


def _interleave(x):
    s, d = x.shape
    x4 = x.reshape(s // TOK_BLOCK, CHUNKS_PER_BLOCK, CHUNK, d)
    return x4.transpose(0, 2, 1, 3).reshape(s, d)


def _deinterleave(x):
    s, d = x.shape
    x4 = x.reshape(s // TOK_BLOCK, CHUNK, CHUNKS_PER_BLOCK, d)
    return x4.transpose(0, 2, 1, 3).reshape(s, d)


def kernel(x, attn_norm_w, w_in, lb_fwd, lb_bwd, hgrn_norm_w, conv_w, w_out, ffn_norm_w,
           w_gate_up, w_down, final_norm_w):
    bsz, s, d = x.shape
    depth = w_in.shape[0]
    d_hgrn = lb_fwd.shape[1]
    d_conv = conv_w.shape[2]
    assert bsz == 1 and s % (BWD_BLOCKS * TOK_BLOCK) == 0
    assert w_in.shape[2] == 5 * d_hgrn + 3 * d_conv and d_hgrn == d_conv

    xs = _interleave(x[0])
    w_in, w_out, w_gate_up, w_down = (_cast_bf16(w) for w in (w_in, w_out, w_gate_up, w_down))
    fw = final_norm_w.reshape(1, d)
    for l in range(depth):
        proj_f, proj_h = _in_proj(xs, attn_norm_w[l].reshape(1, d), w_in, l, tm=1024, tn=d_hgrn)
        o_part, qb, kb, vb, tb = _rec_fwd(proj_f, proj_h, lb_fwd, lb_bwd, l, d_hgrn)
        y_rec = _rec_bwd(qb, kb, vb, tb, proj_h, o_part, hgrn_norm_w[l].reshape(1, d_hgrn), d_hgrn)
        y_conv = _conv(proj_h, conv_w[l], d_hgrn, d_conv, tc=512)
        xs = _out_proj(xs, y_rec, y_conv, w_out, l, tm=512)
        xs = _ffn(xs, ffn_norm_w[l].reshape(1, d), w_gate_up, w_down, fw, l, l == depth - 1,
                  tm=512, tf=512)
    return _deinterleave(xs)[None]
```

```python
import functools

import jax
import jax.numpy as jnp
from jax import lax
from jax.experimental import pallas as pl
from jax.experimental.pallas import tpu as pltpu

F32 = jnp.float32
BF16 = jnp.bfloat16

EPS = 1e-6
HEAD_DIM = 128
CHUNK = 128
SUBLANES = 8
LANES = 128
CHUNKS_PER_BLOCK = SUBLANES
TOK_BLOCK = CHUNK * CHUNKS_PER_BLOCK
POS_BLOCK = SUBLANES
N_GROUPS = CHUNK // POS_BLOCK
GROUP_ROWS = POS_BLOCK * CHUNKS_PER_BLOCK
LOW_LEVELS = tuple(1 << b for b in range(POS_BLOCK.bit_length() - 1))
HIGH_SPANS = tuple(1 << b for b in range(N_GROUPS.bit_length() - 1))
N_LOW = len(LOW_LEVELS) + 1
N_LEVELS = N_LOW + len(HIGH_SPANS)
N_FAC_LEVEL = 2 * (len(HIGH_SPANS) - 1)
FAC_QF, FAC_KF, FAC_QB, FAC_KB = (N_FAC_LEVEL + i for i in range(4))
FAC_RQF, FAC_RKF, FAC_RQB, FAC_RKB = (N_FAC_LEVEL + 4 + i for i in range(4))
N_FAC = N_FAC_LEVEL + 8
FAC_ROWS = N_FAC * N_GROUPS * CHUNKS_PER_BLOCK
BASE_QP, BASE_KS, BASE_QS, BASE_KP, BASE_QI, BASE_KI = range(6)
N_BASE = 6
RATIO_SPAN = 4
RATIO_MIN = 2.0 ** -100
BWD_BLOCKS = 4
GROUP_Q, GROUP_ZF, GROUP_ZB, GROUP_V, GROUP_GATE, GROUP_CB, GROUP_CC, GROUP_CH = range(8)
F32_GROUP0, N_F32_GROUPS = GROUP_ZF, 3
CONV_WIDTH = 3
HALO_ROWS = 2 * SUBLANES
VMEM_LIMIT = 58 * 1024 * 1024
CAST_BLOCK_BYTES = 8 * 1024 * 1024

_NT = (((1,), (1,)), ((), ()))
_TN = (((0,), (0,)), ((), ()))


def _sigmoid(x):
    return 0.5 * jnp.tanh(0.5 * x) + 0.5


def _rms(x, w):
    ms = jnp.mean(x * x, axis=-1, keepdims=True)
    return x * lax.rsqrt(ms + EPS) * w


def _params(sem):
    return pltpu.CompilerParams(dimension_semantics=sem, vmem_limit_bytes=VMEM_LIMIT)


def _cast_kernel(w_ref, o_ref):
    o_ref[...] = w_ref[...].astype(o_ref.dtype)


def _cast_bf16(w):
    depth, r, c = w.shape
    rows = depth * r
    tr = 1 << ((CAST_BLOCK_BYTES // (4 * c)).bit_length() - 1)
    assert tr >= 2 * SUBLANES and rows % tr == 0
    out = pl.pallas_call(
        _cast_kernel,
        grid=(rows // tr,),
        in_specs=[pl.BlockSpec((tr, c), lambda i: (i, 0))],
        out_specs=pl.BlockSpec((tr, c), lambda i: (i, 0)),
        out_shape=jax.ShapeDtypeStruct((rows, c), BF16),
        compiler_params=_params(("parallel",)),
        name="cast_bf16",
    )(w.reshape(rows, c))
    return out.reshape(depth, r, c)


def _in_proj_kernel(x_ref, nw_ref, w_ref, of_ref, oh_ref, h_scr):
    j = pl.program_id(1)

    @pl.when(j == 0)
    def _():
        h_scr[...] = _rms(x_ref[...], nw_ref[...]).astype(BF16)

    def project(o_ref):
        res = jnp.dot(h_scr[...], w_ref[...], preferred_element_type=F32)
        for c in range(o_ref.shape[0]):
            o_ref[c] = res[:, c * LANES:(c + 1) * LANES].astype(o_ref.dtype)

    wide = jnp.logical_and(j >= F32_GROUP0, j < F32_GROUP0 + N_F32_GROUPS)
    pl.when(wide)(functools.partial(project, of_ref))
    pl.when(jnp.logical_not(wide))(functools.partial(project, oh_ref))


def _in_proj(x, nw, w, layer, *, tm, tn):
    s, d = x.shape
    n = w.shape[2]
    ngroups = n // tn
    slabs = tn // LANES
    assert F32_GROUP0 == 1 and ngroups > F32_GROUP0 + N_F32_GROUPS
    nhalf = ngroups - N_F32_GROUPS
    return pl.pallas_call(
        _in_proj_kernel,
        grid=(s // tm, ngroups),
        in_specs=[
            pl.BlockSpec((tm, d), lambda i, j: (i, 0)),
            pl.BlockSpec((1, d), lambda i, j: (0, 0)),
            pl.BlockSpec((None, d, tn), lambda i, j: (layer, 0, j)),
        ],
        out_specs=[
            pl.BlockSpec((slabs, tm, LANES),
                         lambda i, j: (jnp.clip(j - F32_GROUP0, 0, N_F32_GROUPS - 1), i, 0)),
            pl.BlockSpec((slabs, tm, LANES),
                         lambda i, j: (jnp.clip(j - N_F32_GROUPS, 0, nhalf - 1), i, 0)),
        ],
        out_shape=[jax.ShapeDtypeStruct((N_F32_GROUPS * slabs, s, LANES), F32),
                   jax.ShapeDtypeStruct((nhalf * slabs, s, LANES), BF16)],
        scratch_shapes=[pltpu.VMEM((tm, d), BF16)],
        compiler_params=_params(("parallel", "arbitrary")),
        name="in_proj",
    )(x, nw, w)


def _lower_bound(lb_ref, layer):
    p = lb_ref[...]
    if layer == 0:
        return jnp.zeros((1, p.shape[1]), F32)
    e = jnp.exp(p - jnp.max(p, axis=0, keepdims=True))
    sm = e / jnp.sum(e, axis=0, keepdims=True)
    return jnp.sum(sm[1:layer + 1], axis=0, keepdims=True)


def _gates(z, lb):
    th = 0.5 * jnp.tanh(0.5 * z)
    half = 0.5 * (1.0 - lb)
    f = (lb + half) + (1.0 - lb) * th
    k = half - (1.0 - lb) * th
    return f, k


def _halves(x, h):
    x5 = x.reshape((x.shape[0] // (2 * h), 2, h) + x.shape[1:])
    return x5[:, 0], x5[:, 1]


def _join(a, b):
    x = jnp.stack([a, b], axis=1)
    return x.reshape((x.shape[0] * x.shape[1] * x.shape[2],) + a.shape[2:])


def _chunk_rows(ref, r):
    return ref[pl.ds(r, CHUNK, stride=CHUNKS_PER_BLOCK), :]


def _fac_rows(k, g):
    return pl.ds((k * N_GROUPS + g) * CHUNKS_PER_BLOCK, CHUNKS_PER_BLOCK)


def _group_factor(fac_ref, k, r):
    return jnp.concatenate(
        [jnp.broadcast_to(fac_ref[pl.ds((k * N_GROUPS + g) * CHUNKS_PER_BLOCK + r, 1), :],
                          (POS_BLOCK, HEAD_DIM))
         for g in range(N_GROUPS)], axis=0)


def _by_group_parity(even, odd, span):
    pieces = []
    for g in range(N_GROUPS):
        src = odd if (g // span) % 2 else even
        pieces.append(src[g * POS_BLOCK:(g + 1) * POS_BLOCK])
    return jnp.concatenate(pieces, axis=0)


def _seg_products(tot, span):
    ones = jnp.ones_like(tot[0])
    pre, suf = [None] * N_GROUPS, [None] * N_GROUPS
    for s0 in range(0, N_GROUPS, span):
        acc = None
        for g in range(s0, s0 + span):
            pre[g] = ones if acc is None else acc
            acc = tot[g] if acc is None else acc * tot[g]
        acc = None
        for g in reversed(range(s0, s0 + span)):
            suf[g] = ones if acc is None else acc
            acc = tot[g] if acc is None else acc * tot[g]
    return pre, suf


def _group_scan(g, q_ref, zf_ref, zb_ref, lbf, lbb, low_stage=None):
    sh = (POS_BLOCK, CHUNKS_PER_BLOCK, HEAD_DIM)
    flat = (GROUP_ROWS, HEAD_DIM)
    rows = pl.ds(g * GROUP_ROWS, GROUP_ROWS)
    q = q_ref[rows, :].astype(F32).reshape(sh)
    qs = q * _sigmoid(q)
    f_f, k_f = _gates(zf_ref[rows, :].reshape(sh), lbf)
    f_b, k_b = _gates(zb_ref[rows, :].reshape(sh), lbb)
    if low_stage is not None:
        stq_ref, stk_ref = low_stage
        stq_ref[0, rows, :] = qs.reshape(flat)
        stk_ref[0, rows, :] = (k_f + k_b).reshape(flat)
    pin, sex, tf = f_f, jnp.ones(sh, F32), f_f
    sin, pex, tb = f_b, jnp.ones(sh, F32), f_b
    for li, h in enumerate(LOW_LEVELS):
        pin1, pin2 = _halves(pin, h)
        sex1, sex2 = _halves(sex, h)
        sin1, sin2 = _halves(sin, h)
        pex1, pex2 = _halves(pex, h)
        if low_stage is not None:
            qs1, qs2 = _halves(qs, h)
            kf1, _ = _halves(k_f, h)
            _, kb2 = _halves(k_b, h)
            stq_ref[li + 1, rows, :] = _join(qs1 * sin1, qs2 * pin2).reshape(flat)
            stk_ref[li + 1, rows, :] = _join(kf1 * sex1, kb2 * pex2).reshape(flat)
        tf1, tf2 = _halves(tf, 1)
        tb1, tb2 = _halves(tb, 1)
        pin = _join(pin1, pin2 * tf1)
        sex = _join(sex1 * tf2, sex2)
        sin = _join(sin1 * tb2, sin2)
        pex = _join(pex1, pex2 * tb1)
        tf = (tf1 * tf2).reshape((tf.shape[0] // 2,) + sh[1:])
        tb = (tb1 * tb2).reshape((tb.shape[0] // 2,) + sh[1:])
    return rows, qs, k_f, k_b, pin, sex, sin, pex, tf[0], tb[0]


def _chunk_factors(tot_f, tot_b, fac_ref, tf_ref, tb_ref):
    pre_f, suf_f = _seg_products(tot_f, N_GROUPS)
    pre_b, suf_b = _seg_products(tot_b, N_GROUPS)
    for g in range(N_GROUPS):
        fac_ref[_fac_rows(FAC_QF, g), :] = pre_f[g]
        fac_ref[_fac_rows(FAC_KF, g), :] = suf_f[g]
        fac_ref[_fac_rows(FAC_QB, g), :] = suf_b[g]
        fac_ref[_fac_rows(FAC_KB, g), :] = pre_b[g]
    tf_ref[...] = pre_f[N_GROUPS - 1] * tot_f[N_GROUPS - 1]
    tb_ref[...] = pre_b[N_GROUPS - 1] * tot_b[N_GROUPS - 1]


def _level_factors(tot_f, tot_b, fac_ref, si):
    span = HIGH_SPANS[si]
    pre_f, suf_f = _seg_products(tot_f, span)
    pre_b, suf_b = _seg_products(tot_b, span)
    for g in range(N_GROUPS):
        second = (g // span) % 2
        fac_ref[_fac_rows(2 * (si - 1), g), :] = pre_f[g] if second else suf_b[g]
        fac_ref[_fac_rows(2 * (si - 1) + 1, g), :] = pre_b[g] if second else suf_f[g]


def _scan_ratio(layer, lbf_ref, lbb_ref, q_ref, zf_ref, zb_ref, tb_ref, stage):
    _, _, base_ref, fac_ref, tf_ref = stage
    lbf = _lower_bound(lbf_ref, layer)
    lbb = _lower_bound(lbb_ref, layer)
    flat = (GROUP_ROWS, HEAD_DIM)
    tot_f, tot_b = [], []
    for g in range(N_GROUPS):
        rows, qs, k_f, k_b, pin, sex, sin, pex, tf, tb = _group_scan(
            g, q_ref, zf_ref, zb_ref, lbf, lbb)
        base_ref[BASE_QP, rows, :] = (qs * pin).reshape(flat)
        base_ref[BASE_KS, rows, :] = (k_f * sex).reshape(flat)
        base_ref[BASE_QS, rows, :] = (qs * sin).reshape(flat)
        base_ref[BASE_KP, rows, :] = (k_b * pex).reshape(flat)
        if (g // RATIO_SPAN) % 2:
            base_ref[BASE_QI, rows, :] = (qs * (1.0 / pex)).reshape(flat)
            base_ref[BASE_KI, rows, :] = (k_f * (1.0 / pin)).reshape(flat)
        else:
            base_ref[BASE_QI, rows, :] = (qs * (1.0 / sex)).reshape(flat)
            base_ref[BASE_KI, rows, :] = (k_b * (1.0 / sin)).reshape(flat)
        tot_f.append(tf)
        tot_b.append(tb)

    pre_f, suf_f = _seg_products(tot_f, RATIO_SPAN)
    pre_b, suf_b = _seg_products(tot_b, RATIO_SPAN)
    smallest = None
    for g in range(N_GROUPS):
        second = (g // RATIO_SPAN) % 2
        fac_ref[_fac_rows(FAC_RQF, g), :] = pre_f[g] if second else 1.0 / suf_f[g]
        fac_ref[_fac_rows(FAC_RKF, g), :] = 1.0 / pre_f[g] if second else suf_f[g]
        fac_ref[_fac_rows(FAC_RQB, g), :] = 1.0 / pre_b[g] if second else suf_b[g]
        fac_ref[_fac_rows(FAC_RKB, g), :] = pre_b[g] if second else 1.0 / suf_b[g]
        if g % RATIO_SPAN == RATIO_SPAN - 1:
            run = jnp.minimum(pre_f[g] * tot_f[g], pre_b[g] * tot_b[g])
            smallest = run if smallest is None else jnp.minimum(smallest, run)
    _level_factors(tot_f, tot_b, fac_ref, len(HIGH_SPANS) - 1)
    _chunk_factors(tot_f, tot_b, fac_ref, tf_ref, tb_ref)
    return jnp.min(smallest)


def _scan_levels(layer, lbf_ref, lbb_ref, q_ref, zf_ref, zb_ref, tb_ref, stage):
    stq_ref, stk_ref, base_ref, fac_ref, tf_ref = stage
    lbf = _lower_bound(lbf_ref, layer)
    lbb = _lower_bound(lbb_ref, layer)
    flat = (GROUP_ROWS, HEAD_DIM)
    tot_f, tot_b = [], []
    for g in range(N_GROUPS):
        rows, qs, k_f, k_b, pin, sex, sin, pex, tf, tb = _group_scan(
            g, q_ref, zf_ref, zb_ref, lbf, lbb, (stq_ref, stk_ref))
        base_ref[BASE_QP, rows, :] = (qs * pin).reshape(flat)
        base_ref[BASE_KS, rows, :] = (k_f * sex).reshape(flat)
        base_ref[BASE_QS, rows, :] = (qs * sin).reshape(flat)
        base_ref[BASE_KP, rows, :] = (k_b * pex).reshape(flat)
        tot_f.append(tf)
        tot_b.append(tb)
    for si in range(1, len(HIGH_SPANS)):
        _level_factors(tot_f, tot_b, fac_ref, si)
    _chunk_factors(tot_f, tot_b, fac_ref, tf_ref, tb_ref)


def _chunk_bases(base_ref, r):
    return tuple(_chunk_rows(base_ref.at[k], r) for k in (BASE_QP, BASE_KS, BASE_QS, BASE_KP))


def _high_level(bases, fac_ref, si, r):
    qp, ks, qs, kp = bases
    span = HIGH_SPANS[si]
    ql = _by_group_parity(qs, qp, span)
    kl = _by_group_parity(ks, kp, span)
    if si > 0:
        ql = ql * _group_factor(fac_ref, 2 * (si - 1), r)
        kl = kl * _group_factor(fac_ref, 2 * (si - 1) + 1, r)
    return ql.astype(BF16), kl.astype(BF16)


def _nt_dot(a, b):
    return lax.dot_general(a, b, _NT, preferred_element_type=F32)


def _finish_chunk(r, a, bases, v_ref, o_ref, bwd_refs, st_ref, fac_ref, tf_ref):
    qp, ks, qs, kp = bases
    vb = _chunk_rows(v_ref, r).astype(BF16)
    st = st_ref[...]
    o = jnp.dot(a.astype(BF16), vb, preferred_element_type=F32)
    qf = (qp * _group_factor(fac_ref, FAC_QF, r)).astype(BF16)
    o = o + _nt_dot(qf, st.astype(BF16))
    kf = (ks * _group_factor(fac_ref, FAC_KF, r)).astype(BF16)
    upd = lax.dot_general(vb, kf, _TN, preferred_element_type=F32)
    st_ref[...] = st * tf_ref[pl.ds(r, 1), :] + upd
    o_ref[pl.ds(r, CHUNK, stride=CHUNKS_PER_BLOCK), :] = o
    rows = pl.ds(r * CHUNK, CHUNK)
    qb_ref, kb_ref, vb_ref = bwd_refs
    qb_ref[rows, :] = (qs * _group_factor(fac_ref, FAC_QB, r)).astype(BF16)
    kb_ref[rows, :] = (kp * _group_factor(fac_ref, FAC_KB, r)).astype(BF16)
    vb_ref[rows, :] = vb


def _matmul_ratio(v_ref, o_ref, bwd_refs, st_ref, stage):
    _, _, base_ref, fac_ref, tf_ref = stage
    row = lax.broadcasted_iota(jnp.int32, (CHUNK, CHUNK), 0)
    col = lax.broadcasted_iota(jnp.int32, (CHUNK, CHUNK), 1)
    far = (row ^ col) >= 2 * RATIO_SPAN * POS_BLOCK
    top = len(HIGH_SPANS) - 1
    for r in range(CHUNKS_PER_BLOCK):
        bases = _chunk_bases(base_ref, r)
        qp, ks, qs, kp = bases
        qi = _chunk_rows(base_ref.at[BASE_QI], r)
        ki = _chunk_rows(base_ref.at[BASE_KI], r)
        qf = _by_group_parity(qi, qp, RATIO_SPAN) * _group_factor(fac_ref, FAC_RQF, r)
        kf = _by_group_parity(ks, ki, RATIO_SPAN) * _group_factor(fac_ref, FAC_RKF, r)
        qb = _by_group_parity(qs, qi, RATIO_SPAN) * _group_factor(fac_ref, FAC_RQB, r)
        kb = _by_group_parity(ki, kp, RATIO_SPAN) * _group_factor(fac_ref, FAC_RKB, r)
        p_f = _nt_dot(qf.astype(BF16), kf.astype(BF16))
        p_b = _nt_dot(qb.astype(BF16), kb.astype(BF16))
        p_top = _nt_dot(*_high_level(bases, fac_ref, top, r))
        a = jnp.where(far, p_top, jnp.where(row >= col, p_f, p_b))
        a = a + jnp.where(row == col, p_b, 0.0)
        _finish_chunk(r, a, bases, v_ref, o_ref, bwd_refs, st_ref, fac_ref, tf_ref)


def _matmul_levels(v_ref, o_ref, bwd_refs, st_ref, stage):
    stq_ref, stk_ref, base_ref, fac_ref, tf_ref = stage
    row = lax.broadcasted_iota(jnp.int32, (CHUNK, CHUNK), 0)
    col = lax.broadcasted_iota(jnp.int32, (CHUNK, CHUNK), 1)
    xr = row ^ col
    lvl = jnp.zeros((CHUNK, CHUNK), jnp.int32)
    for b in range(N_LEVELS - 1):
        lvl = lvl + jnp.where(xr >= (1 << b), 1, 0)

    for r in range(CHUNKS_PER_BLOCK):
        bases = _chunk_bases(base_ref, r)
        a = None
        for li in range(N_LEVELS):
            if li < N_LOW:
                ql = _chunk_rows(stq_ref.at[li], r).astype(BF16)
                kl = _chunk_rows(stk_ref.at[li], r).astype(BF16)
            else:
                ql, kl = _high_level(bases, fac_ref, li - N_LOW, r)
            p = _nt_dot(ql, kl)
            a = p if a is None else jnp.where(lvl == li, p, a)
        _finish_chunk(r, a, bases, v_ref, o_ref, bwd_refs, st_ref, fac_ref, tf_ref)


def _rec_fwd_kernel(layer, lbf_ref, lbb_ref, q_ref, zf_ref, zb_ref, v_ref,
                    o_ref, qb_ref, kb_ref, vb_ref, tb_ref, *scratch):
    stage, st_ref = scratch[:-1], scratch[-1]
    bwd_refs = (qb_ref, kb_ref, vb_ref)

    @pl.when(pl.program_id(1) == 0)
    def _():
        st_ref[...] = jnp.zeros_like(st_ref)

    smallest = _scan_ratio(layer, lbf_ref, lbb_ref, q_ref, zf_ref, zb_ref, tb_ref, stage)
    safe = smallest >= RATIO_MIN

    @pl.when(safe)
    def _():
        _matmul_ratio(v_ref, o_ref, bwd_refs, st_ref, stage)

    @pl.when(jnp.logical_not(safe))
    def _():
        _scan_levels(layer, lbf_ref, lbb_ref, q_ref, zf_ref, zb_ref, tb_ref, stage)
        _matmul_levels(v_ref, o_ref, bwd_refs, st_ref, stage)


def _group_index(group):
    if F32_GROUP0 <= group < F32_GROUP0 + N_F32_GROUPS:
        return group - F32_GROUP0
    return group if group < F32_GROUP0 else group - N_F32_GROUPS


def _rec_fwd(proj_f, proj_h, lb_fwd, lb_bwd, layer, d_hgrn):
    s = proj_f.shape[1]
    depth = lb_fwd.shape[0]
    nh = d_hgrn // HEAD_DIM
    nblk = s // TOK_BLOCK
    blk = (TOK_BLOCK, HEAD_DIM)
    hblk = (None,) + blk
    here = lambda h, i: (h, i, 0)

    def col(group):
        return pl.BlockSpec(hblk, lambda h, i, g=_group_index(group): (g * nh + h, i, 0))

    lb_spec = pl.BlockSpec((depth, HEAD_DIM), lambda h, i: (0, h))
    return pl.pallas_call(
        functools.partial(_rec_fwd_kernel, layer),
        grid=(nh, nblk),
        in_specs=[lb_spec, lb_spec, col(GROUP_Q), col(GROUP_ZF), col(GROUP_ZB), col(GROUP_V)],
        out_specs=[
            pl.BlockSpec(hblk, here),
            pl.BlockSpec(hblk, here),
            pl.BlockSpec(hblk, here),
            pl.BlockSpec(hblk, here),
            pl.BlockSpec((None, CHUNKS_PER_BLOCK, HEAD_DIM), here),
        ],
        out_shape=[jax.ShapeDtypeStruct((nh, s, HEAD_DIM), F32),
                   jax.ShapeDtypeStruct((nh, s, HEAD_DIM), BF16),
                   jax.ShapeDtypeStruct((nh, s, HEAD_DIM), BF16),
                   jax.ShapeDtypeStruct((nh, s, HEAD_DIM), BF16),
                   jax.ShapeDtypeStruct((nh, nblk * CHUNKS_PER_BLOCK, HEAD_DIM), F32)],
        scratch_shapes=[
            pltpu.VMEM((N_LOW,) + blk, F32),
            pltpu.VMEM((N_LOW,) + blk, F32),
            pltpu.VMEM((N_BASE,) + blk, F32),
            pltpu.VMEM((FAC_ROWS, HEAD_DIM), F32),
            pltpu.VMEM((CHUNKS_PER_BLOCK, HEAD_DIM), F32),
            pltpu.VMEM((HEAD_DIM, HEAD_DIM), F32),
        ],
        compiler_params=_params(("parallel", "arbitrary")),
        name="rec_fwd",
    )(lb_fwd, lb_bwd, proj_h, proj_f, proj_f, proj_f)


def _rec_bwd_kernel(qb_ref, kb_ref, vb_ref, tb_ref, op_ref, g_ref, nw_ref, y_ref, ob_ref, st_ref):
    @pl.when(pl.program_id(1) == 0)
    def _():
        st_ref[...] = jnp.zeros_like(st_ref)

    for blk in reversed(range(BWD_BLOCKS)):
        for r in reversed(range(CHUNKS_PER_BLOCK)):
            st = st_ref[...]
            rows = pl.ds(blk * TOK_BLOCK + r * CHUNK, CHUNK)
            strided = pl.ds(blk * TOK_BLOCK + r, CHUNK, stride=CHUNKS_PER_BLOCK)
            ob_ref[strided, :] = _nt_dot(qb_ref[rows, :], st.astype(BF16))
            upd = lax.dot_general(vb_ref[rows, :], kb_ref[rows, :], _TN,
                                  preferred_element_type=F32)
            st_ref[...] = st * tb_ref[pl.ds(blk * CHUNKS_PER_BLOCK + r, 1), :] + upd
    g = g_ref[...].astype(F32)
    y = _rms(op_ref[...] + ob_ref[...], nw_ref[...]) * (g * _sigmoid(g))
    y_ref[...] = y.astype(y_ref.dtype)


def _rec_bwd(qb, kb, vb, tb, proj_h, o_part, norm_w, d_hgrn):
    s = proj_h.shape[1]
    nh = d_hgrn // HEAD_DIM
    rows = BWD_BLOCKS * TOK_BLOCK
    nstep = s // rows
    blk = (rows, HEAD_DIM)
    hblk = (None,) + blk
    rev = lambda h, i: (h, nstep - 1 - i, 0)

    def col(group):
        return pl.BlockSpec(hblk, lambda h, i, g=_group_index(group): (g * nh + h, nstep - 1 - i, 0))

    return pl.pallas_call(
        _rec_bwd_kernel,
        grid=(nh, nstep),
        in_specs=[
            pl.BlockSpec(hblk, rev),
            pl.BlockSpec(hblk, rev),
            pl.BlockSpec(hblk, rev),
            pl.BlockSpec((None, BWD_BLOCKS * CHUNKS_PER_BLOCK, HEAD_DIM), rev),
            pl.BlockSpec(hblk, rev),
            col(GROUP_GATE),
            pl.BlockSpec((1, HEAD_DIM), lambda h, i: (0, h)),
        ],
        out_specs=pl.BlockSpec(blk, lambda h, i: (nstep - 1 - i, h)),
        out_shape=jax.ShapeDtypeStruct((s, d_hgrn), BF16),
        scratch_shapes=[pltpu.VMEM(blk, F32), pltpu.VMEM((HEAD_DIM, HEAD_DIM), F32)],
        compiler_params=_params(("parallel", "arbitrary")),
        name="rec_bwd",
    )(qb, kb, vb, tb, o_part, proj_h, norm_w)


def _conv_kernel(b_ref, c_ref, h_ref, cp_ref, hp_ref, cn_ref, hn_ref, w_ref, y_ref):
    i = pl.program_id(0)
    last = pl.num_programs(0) - 1
    n = c_ref.shape[1]
    sub = lax.broadcasted_iota(jnp.int32, (SUBLANES, LANES), 0)
    for c in range(c_ref.shape[0]):
        u = c_ref[c].astype(F32) * h_ref[c].astype(F32)
        halo_p = (cp_ref[c].astype(F32) * hp_ref[c].astype(F32))[HALO_ROWS - SUBLANES:]
        halo_p = jnp.where(i > 0, halo_p, 0.0)
        first = jnp.where(sub == 0, pltpu.roll(halo_p, 1, 0), pltpu.roll(u[n - SUBLANES:], 1, 0))
        u_prev = jnp.concatenate([first, u[:n - SUBLANES]], axis=0)
        halo_n = (cn_ref[c].astype(F32) * hn_ref[c].astype(F32))[:SUBLANES]
        halo_n = jnp.where(i < last, halo_n, 0.0)
        lastv = jnp.where(sub == SUBLANES - 1, pltpu.roll(halo_n, SUBLANES - 1, 0),
                          pltpu.roll(u[:SUBLANES], SUBLANES - 1, 0))
        u_next = jnp.concatenate([u[SUBLANES:], lastv], axis=0)
        w = w_ref[:, c * LANES:(c + 1) * LANES]
        y = w[0:1] * u_prev + w[1:2] * u + w[2:3] * u_next
        y_ref[:, c * LANES:(c + 1) * LANES] = (b_ref[c].astype(F32) * y).astype(y_ref.dtype)


def _conv(proj_h, conv_w, d_hgrn, d_conv, *, tc):
    s = proj_h.shape[1]
    nblk = s // TOK_BLOCK
    nhalo = s // HALO_ROWS
    per_blk = TOK_BLOCK // HALO_ROWS
    slabs = tc // LANES
    ncol = d_conv // tc

    def first_block(group):
        return _group_index(group) * (d_hgrn // tc)

    def main(group):
        return pl.BlockSpec((slabs, TOK_BLOCK, LANES),
                            lambda i, j, b=first_block(group): (b + j, i, 0))

    def prev(group):
        return pl.BlockSpec((slabs, HALO_ROWS, LANES),
                            lambda i, j, b=first_block(group): (b + j,
                                                                jnp.maximum(i * per_blk - 1, 0), 0))

    def nxt(group):
        return pl.BlockSpec((slabs, HALO_ROWS, LANES),
                            lambda i, j, b=first_block(group): (
                                b + j, jnp.minimum((i + 1) * per_blk, nhalo - 1), 0))

    return pl.pallas_call(
        _conv_kernel,
        grid=(nblk, ncol),
        in_specs=[main(GROUP_CB), main(GROUP_CC), main(GROUP_CH), prev(GROUP_CC), prev(GROUP_CH),
                  nxt(GROUP_CC), nxt(GROUP_CH),
                  pl.BlockSpec((CONV_WIDTH, tc), lambda i, j: (0, j))],
        out_specs=pl.BlockSpec((TOK_BLOCK, tc), lambda i, j: (i, j)),
        out_shape=jax.ShapeDtypeStruct((s, d_conv), BF16),
        compiler_params=_params(("parallel", "parallel")),
        name="short_conv",
    )(proj_h, proj_h, proj_h, proj_h, proj_h, proj_h, proj_h, conv_w)


def _out_proj_kernel(x_ref, yr_ref, yc_ref, wr_ref, wc_ref, o_ref):
    acc = jnp.dot(yr_ref[...], wr_ref[...], preferred_element_type=F32)
    acc = acc + jnp.dot(yc_ref[...], wc_ref[...], preferred_element_type=F32)
    o_ref[...] = x_ref[...] + acc


def _out_proj(x, y_rec, y_conv, w, layer, *, tm):
    s, d = x.shape
    d_rec = y_rec.shape[1]
    d_conv = y_conv.shape[1]
    assert d_rec == d_conv
    return pl.pallas_call(
        _out_proj_kernel,
        grid=(s // tm,),
        in_specs=[
            pl.BlockSpec((tm, d), lambda i: (i, 0)),
            pl.BlockSpec((tm, d_rec), lambda i: (i, 0)),
            pl.BlockSpec((tm, d_conv), lambda i: (i, 0)),
            pl.BlockSpec((None, d_rec, d), lambda i: (layer, 0, 0)),
            pl.BlockSpec((None, d_conv, d), lambda i: (layer, 1, 0)),
        ],
        out_specs=pl.BlockSpec((tm, d), lambda i: (i, 0)),
        out_shape=jax.ShapeDtypeStruct((s, d), F32),
        compiler_params=_params(("parallel",)),
        name="out_proj",
    )(x, y_rec, y_conv, w, w)


def _ffn_kernel(final, x_ref, nw_ref, wg_ref, wu_ref, wd_ref, fw_ref, o_ref, h_scr):
    f = pl.program_id(1)

    @pl.when(f == 0)
    def _():
        x = x_ref[...]
        h_scr[...] = _rms(x, nw_ref[...]).astype(BF16)
        o_ref[...] = x

    h = h_scr[...]
    g = jnp.dot(h, wg_ref[...], preferred_element_type=F32)
    u = jnp.dot(h, wu_ref[...], preferred_element_type=F32)
    a = (g * _sigmoid(g) * u).astype(BF16)
    o_ref[...] += jnp.dot(a, wd_ref[...], preferred_element_type=F32)

    if final:
        @pl.when(f == pl.num_programs(1) - 1)
        def _():
            o_ref[...] = _rms(o_ref[...], fw_ref[...])


def _ffn(x, nw, w_gu, w_d, fw, layer, final, *, tm, tf):
    s, d = x.shape
    d_ff = w_d.shape[1]
    nf = d_ff // tf
    return pl.pallas_call(
        functools.partial(_ffn_kernel, final),
        grid=(s // tm, nf),
        in_specs=[
            pl.BlockSpec((tm, d), lambda i, f: (i, 0)),
            pl.BlockSpec((1, d), lambda i, f: (0, 0)),
            pl.BlockSpec((None, d, tf), lambda i, f: (layer, 0, f)),
            pl.BlockSpec((None, d, tf), lambda i, f: (layer, 0, nf + f)),
            pl.BlockSpec((None, tf, d), lambda i, f: (layer, f, 0)),
            pl.BlockSpec((1, d), lambda i, f: (0, 0)),
        ],
        out_specs=pl.BlockSpec((tm, d), lambda i, f: (i, 0)),
        out_shape=jax.ShapeDtypeStruct((s, d), F32),
        scratch_shapes=[pltpu.VMEM((tm, d), BF16)],
        compiler_params=_params(("parallel", "arbitrary")),
        name="ffn",
    )(x, nw, w_gu, w_gu, w_d, fw)


def _interleave(x):
    s, d = x.shape
    x4 = x.reshape(s // TOK_BLOCK, CHUNKS_PER_BLOCK, CHUNK, d)
    return x4.transpose(0, 2, 1, 3).reshape(s, d)


def _deinterleave(x):
    s, d = x.shape
    x4 = x.reshape(s // TOK_BLOCK, CHUNK, CHUNKS_PER_BLOCK, d)
    return x4.transpose(0, 2, 1, 3).reshape(s, d)


def kernel(x, attn_norm_w, w_in, lb_fwd, lb_bwd, hgrn_norm_w, conv_w, w_out, ffn_norm_w,
           w_gate_up, w_down, final_norm_w):
    bsz, s, d = x.shape
    depth = w_in.shape[0]
    d_hgrn = lb_fwd.shape[1]
    d_conv = conv_w.shape[2]
    assert bsz == 1 and s % (BWD_BLOCKS * TOK_BLOCK) == 0
    assert w_in.shape[2] == 5 * d_hgrn + 3 * d_conv and d_hgrn == d_conv

    xs = _interleave(x[0])
    w_in, w_out, w_gate_up, w_down = (_cast_bf16(w) for w in (w_in, w_out, w_gate_up, w_down))
    fw = final_norm_w.reshape(1, d)
    for l in range(depth):
        proj_f, proj_h = _in_proj(xs, attn_norm_w[l].reshape(1, d), w_in, l, tm=1024, tn=d_hgrn)
        o_part, qb, kb, vb, tb = _rec_fwd(proj_f, proj_h, lb_fwd, lb_bwd, l, d_hgrn)
        y_rec = _rec_bwd(qb, kb, vb, tb, proj_h, o_part, hgrn_norm_w[l].reshape(1, d_hgrn), d_hgrn)
        y_conv = _conv(proj_h, conv_w[l], d_hgrn, d_conv, tc=512)
        xs = _out_proj(xs, y_rec, y_conv, w_out, l, tm=512)
        xs = _ffn(xs, ffn_norm_w[l].reshape(1, d), w_gate_up, w_down, fw, l, l == depth - 1,
                  tm=1024, tf=512)
    return _deinterleave(xs)[None]
```

```python
import functools

import jax
import jax.numpy as jnp
from jax import lax
from jax.experimental import pallas as pl
from jax.experimental.pallas import tpu as pltpu

F32 = jnp.float32
BF16 = jnp.bfloat16

EPS = 1e-6
HEAD_DIM = 128
CHUNK = 128
SUBLANES = 8
LANES = 128
CHUNKS_PER_BLOCK = SUBLANES
TOK_BLOCK = CHUNK * CHUNKS_PER_BLOCK
POS_BLOCK = SUBLANES
N_GROUPS = CHUNK // POS_BLOCK
GROUP_ROWS = POS_BLOCK * CHUNKS_PER_BLOCK
LOW_LEVELS = tuple(1 << b for b in range(POS_BLOCK.bit_length() - 1))
HIGH_SPANS = tuple(1 << b for b in range(N_GROUPS.bit_length() - 1))
N_LOW = len(LOW_LEVELS) + 1
N_LEVELS = N_LOW + len(HIGH_SPANS)
N_FAC_LEVEL = 2 * (len(HIGH_SPANS) - 1)
FAC_QF, FAC_KF, FAC_QB, FAC_KB = (N_FAC_LEVEL + i for i in range(4))
FAC_RQF, FAC_RKF, FAC_RQB, FAC_RKB = (N_FAC_LEVEL + 4 + i for i in range(4))
N_FAC = N_FAC_LEVEL + 8
FAC_ROWS = N_FAC * N_GROUPS * CHUNKS_PER_BLOCK
BASE_QP, BASE_KS, BASE_QS, BASE_KP, BASE_QI, BASE_KI = range(6)
N_BASE = 6
RATIO_SPAN_FIRST = 4
RATIO_SPAN_LATER = N_GROUPS // 2
RATIO_MIN = 2.0 ** -100
BWD_BLOCKS = 4
GROUP_Q, GROUP_ZF, GROUP_ZB, GROUP_V, GROUP_GATE, GROUP_CB, GROUP_CC, GROUP_CH = range(8)
F32_GROUP0, N_F32_GROUPS = GROUP_ZF, 3
NORM_SPLIT = 4
CONV_WIDTH = 3
HALO_ROWS = 2 * SUBLANES
VMEM_LIMIT = 58 * 1024 * 1024
CAST_BLOCK_BYTES = 8 * 1024 * 1024

_NT = (((1,), (1,)), ((), ()))
_TN = (((0,), (0,)), ((), ()))


def _sigmoid(x):
    return 0.5 * jnp.tanh(0.5 * x) + 0.5


def _rms(x, w):
    ms = jnp.mean(x * x, axis=-1, keepdims=True)
    return x * lax.rsqrt(ms + EPS) * w


def _params(sem):
    return pltpu.CompilerParams(dimension_semantics=sem, vmem_limit_bytes=VMEM_LIMIT)


def _cast_kernel(w_ref, o_ref):
    o_ref[...] = w_ref[...].astype(o_ref.dtype)


def _cast_bf16(w):
    depth, r, c = w.shape
    rows = depth * r
    tr = 1 << ((CAST_BLOCK_BYTES // (4 * c)).bit_length() - 1)
    assert tr >= 2 * SUBLANES and rows % tr == 0
    out = pl.pallas_call(
        _cast_kernel,
        grid=(rows // tr,),
        in_specs=[pl.BlockSpec((tr, c), lambda i: (i, 0))],
        out_specs=pl.BlockSpec((tr, c), lambda i: (i, 0)),
        out_shape=jax.ShapeDtypeStruct((rows, c), BF16),
        compiler_params=_params(("parallel",)),
        name="cast_bf16",
    )(w.reshape(rows, c))
    return out.reshape(depth, r, c)


def _in_proj_kernel(x_ref, nw_ref, w_ref, of_ref, oh_ref, h_scr):
    j = pl.program_id(1)

    def project(o_ref, h, sl):
        res = jnp.dot(h, w_ref[...], preferred_element_type=F32)
        for c in range(o_ref.shape[0]):
            o_ref[c, sl, :] = res[:, c * LANES:(c + 1) * LANES].astype(o_ref.dtype)

    @pl.when(j == 0)
    def _():
        rows = x_ref.shape[0] // NORM_SPLIT
        for m in range(NORM_SPLIT):
            sl = pl.ds(m * rows, rows)
            h = _rms(x_ref[sl, :], nw_ref[...]).astype(BF16)
            h_scr[sl, :] = h
            project(oh_ref, h, sl)

    wide = jnp.logical_and(j >= F32_GROUP0, j < F32_GROUP0 + N_F32_GROUPS)
    narrow = jnp.logical_and(j > 0, jnp.logical_not(wide))
    pl.when(wide)(lambda: project(of_ref, h_scr[...], slice(None)))
    pl.when(narrow)(lambda: project(oh_ref, h_scr[...], slice(None)))


def _in_proj(x, nw, w, layer, *, tm, tn):
    s, d = x.shape
    n = w.shape[2]
    ngroups = n // tn
    slabs = tn // LANES
    assert F32_GROUP0 == 1 and ngroups > F32_GROUP0 + N_F32_GROUPS
    nhalf = ngroups - N_F32_GROUPS
    return pl.pallas_call(
        _in_proj_kernel,
        grid=(s // tm, ngroups),
        in_specs=[
            pl.BlockSpec((tm, d), lambda i, j: (i, 0)),
            pl.BlockSpec((1, d), lambda i, j: (0, 0)),
            pl.BlockSpec((None, d, tn), lambda i, j: (layer, 0, j)),
        ],
        out_specs=[
            pl.BlockSpec((slabs, tm, LANES),
                         lambda i, j: (jnp.clip(j - F32_GROUP0, 0, N_F32_GROUPS - 1), i, 0)),
            pl.BlockSpec((slabs, tm, LANES),
                         lambda i, j: (jnp.clip(j - N_F32_GROUPS, 0, nhalf - 1), i, 0)),
        ],
        out_shape=[jax.ShapeDtypeStruct((N_F32_GROUPS * slabs, s, LANES), F32),
                   jax.ShapeDtypeStruct((nhalf * slabs, s, LANES), BF16)],
        scratch_shapes=[pltpu.VMEM((tm, d), BF16)],
        compiler_params=_params(("parallel", "arbitrary")),
        name="in_proj",
    )(x, nw, w)


def _lower_bound(lb_ref, layer):
    p = lb_ref[...]
    if layer == 0:
        return jnp.zeros((1, p.shape[1]), F32)
    e = jnp.exp(p - jnp.max(p, axis=0, keepdims=True))
    sm = e / jnp.sum(e, axis=0, keepdims=True)
    return jnp.sum(sm[1:layer + 1], axis=0, keepdims=True)


def _gates(z, lb):
    th = 0.5 * jnp.tanh(0.5 * z)
    half = 0.5 * (1.0 - lb)
    f = (lb + half) + (1.0 - lb) * th
    k = half - (1.0 - lb) * th
    return f, k


def _halves(x, h):
    x5 = x.reshape((x.shape[0] // (2 * h), 2, h) + x.shape[1:])
    return x5[:, 0], x5[:, 1]


def _join(a, b):
    x = jnp.stack([a, b], axis=1)
    return x.reshape((x.shape[0] * x.shape[1] * x.shape[2],) + a.shape[2:])


def _chunk_rows(ref, r):
    return ref[pl.ds(r, CHUNK, stride=CHUNKS_PER_BLOCK), :]


def _fac_rows(k, g):
    return pl.ds((k * N_GROUPS + g) * CHUNKS_PER_BLOCK, CHUNKS_PER_BLOCK)


def _group_factor(fac_ref, k, r):
    return jnp.concatenate(
        [jnp.broadcast_to(fac_ref[pl.ds((k * N_GROUPS + g) * CHUNKS_PER_BLOCK + r, 1), :],
                          (POS_BLOCK, HEAD_DIM))
         for g in range(N_GROUPS)], axis=0)


def _by_group_parity(even, odd, span):
    pieces = []
    for g in range(N_GROUPS):
        src = odd if (g // span) % 2 else even
        pieces.append(src[g * POS_BLOCK:(g + 1) * POS_BLOCK])
    return jnp.concatenate(pieces, axis=0)


def _seg_products(tot, span):
    ones = jnp.ones_like(tot[0])
    pre, suf = [None] * N_GROUPS, [None] * N_GROUPS
    for s0 in range(0, N_GROUPS, span):
        acc = None
        for g in range(s0, s0 + span):
            pre[g] = ones if acc is None else acc
            acc = tot[g] if acc is None else acc * tot[g]
        acc = None
        for g in reversed(range(s0, s0 + span)):
            suf[g] = ones if acc is None else acc
            acc = tot[g] if acc is None else acc * tot[g]
    return pre, suf


def _group_scan(g, q_ref, zf_ref, zb_ref, lbf, lbb, low_stage=None):
    sh = (POS_BLOCK, CHUNKS_PER_BLOCK, HEAD_DIM)
    flat = (GROUP_ROWS, HEAD_DIM)
    rows = pl.ds(g * GROUP_ROWS, GROUP_ROWS)
    q = q_ref[rows, :].astype(F32).reshape(sh)
    qs = q * _sigmoid(q)
    f_f, k_f = _gates(zf_ref[rows, :].reshape(sh), lbf)
    f_b, k_b = _gates(zb_ref[rows, :].reshape(sh), lbb)
    if low_stage is not None:
        stq_ref, stk_ref = low_stage
        stq_ref[0, rows, :] = qs.reshape(flat)
        stk_ref[0, rows, :] = (k_f + k_b).reshape(flat)
    pin, sex, tf = f_f, jnp.ones(sh, F32), f_f
    sin, pex, tb = f_b, jnp.ones(sh, F32), f_b
    for li, h in enumerate(LOW_LEVELS):
        pin1, pin2 = _halves(pin, h)
        sex1, sex2 = _halves(sex, h)
        sin1, sin2 = _halves(sin, h)
        pex1, pex2 = _halves(pex, h)
        if low_stage is not None:
            qs1, qs2 = _halves(qs, h)
            kf1, _ = _halves(k_f, h)
            _, kb2 = _halves(k_b, h)
            stq_ref[li + 1, rows, :] = _join(qs1 * sin1, qs2 * pin2).reshape(flat)
            stk_ref[li + 1, rows, :] = _join(kf1 * sex1, kb2 * pex2).reshape(flat)
        tf1, tf2 = _halves(tf, 1)
        tb1, tb2 = _halves(tb, 1)
        pin = _join(pin1, pin2 * tf1)
        sex = _join(sex1 * tf2, sex2)
        sin = _join(sin1 * tb2, sin2)
        pex = _join(pex1, pex2 * tb1)
        tf = (tf1 * tf2).reshape((tf.shape[0] // 2,) + sh[1:])
        tb = (tb1 * tb2).reshape((tb.shape[0] // 2,) + sh[1:])
    return rows, qs, k_f, k_b, pin, sex, sin, pex, tf[0], tb[0]


def _chunk_factors(tot_f, tot_b, fac_ref, tf_ref, tb_ref):
    pre_f, suf_f = _seg_products(tot_f, N_GROUPS)
    pre_b, suf_b = _seg_products(tot_b, N_GROUPS)
    for g in range(N_GROUPS):
        fac_ref[_fac_rows(FAC_QF, g), :] = pre_f[g]
        fac_ref[_fac_rows(FAC_KF, g), :] = suf_f[g]
        fac_ref[_fac_rows(FAC_QB, g), :] = suf_b[g]
        fac_ref[_fac_rows(FAC_KB, g), :] = pre_b[g]
    tf_ref[...] = pre_f[N_GROUPS - 1] * tot_f[N_GROUPS - 1]
    tb_ref[...] = pre_b[N_GROUPS - 1] * tot_b[N_GROUPS - 1]


def _level_factors(tot_f, tot_b, fac_ref, si):
    span = HIGH_SPANS[si]
    pre_f, suf_f = _seg_products(tot_f, span)
    pre_b, suf_b = _seg_products(tot_b, span)
    for g in range(N_GROUPS):
        second = (g // span) % 2
        fac_ref[_fac_rows(2 * (si - 1), g), :] = pre_f[g] if second else suf_b[g]
        fac_ref[_fac_rows(2 * (si - 1) + 1, g), :] = pre_b[g] if second else suf_f[g]


def _scan_ratio(layer, lbf_ref, lbb_ref, q_ref, zf_ref, zb_ref, tb_ref, stage, span):
    _, _, base_ref, fac_ref, tf_ref = stage
    lbf = _lower_bound(lbf_ref, layer)
    lbb = _lower_bound(lbb_ref, layer)
    flat = (GROUP_ROWS, HEAD_DIM)
    tot_f, tot_b = [], []
    for g in range(N_GROUPS):
        rows, qs, k_f, k_b, pin, sex, sin, pex, tf, tb = _group_scan(
            g, q_ref, zf_ref, zb_ref, lbf, lbb)
        base_ref[BASE_QP, rows, :] = (qs * pin).reshape(flat)
        base_ref[BASE_KS, rows, :] = (k_f * sex).reshape(flat)
        base_ref[BASE_QS, rows, :] = (qs * sin).reshape(flat)
        base_ref[BASE_KP, rows, :] = (k_b * pex).reshape(flat)
        if (g // span) % 2:
            base_ref[BASE_QI, rows, :] = (qs * (1.0 / pex)).reshape(flat)
            base_ref[BASE_KI, rows, :] = (k_f * (1.0 / pin)).reshape(flat)
        else:
            base_ref[BASE_QI, rows, :] = (qs * (1.0 / sex)).reshape(flat)
            base_ref[BASE_KI, rows, :] = (k_b * (1.0 / sin)).reshape(flat)
        tot_f.append(tf)
        tot_b.append(tb)

    pre_f, suf_f = _seg_products(tot_f, span)
    pre_b, suf_b = _seg_products(tot_b, span)
    smallest = None
    for g in range(N_GROUPS):
        second = (g // span) % 2
        fac_ref[_fac_rows(FAC_RQF, g), :] = pre_f[g] if second else 1.0 / suf_f[g]
        fac_ref[_fac_rows(FAC_RKF, g), :] = 1.0 / pre_f[g] if second else suf_f[g]
        fac_ref[_fac_rows(FAC_RQB, g), :] = 1.0 / pre_b[g] if second else suf_b[g]
        fac_ref[_fac_rows(FAC_RKB, g), :] = pre_b[g] if second else 1.0 / suf_b[g]
        if g % span == span - 1:
            run = jnp.minimum(pre_f[g] * tot_f[g], pre_b[g] * tot_b[g])
            smallest = run if smallest is None else jnp.minimum(smallest, run)
    if 2 * span < N_GROUPS:
        _level_factors(tot_f, tot_b, fac_ref, len(HIGH_SPANS) - 1)
    _chunk_factors(tot_f, tot_b, fac_ref, tf_ref, tb_ref)
    return jnp.min(smallest)


def _scan_levels(layer, lbf_ref, lbb_ref, q_ref, zf_ref, zb_ref, tb_ref, stage):
    stq_ref, stk_ref, base_ref, fac_ref, tf_ref = stage
    lbf = _lower_bound(lbf_ref, layer)
    lbb = _lower_bound(lbb_ref, layer)
    flat = (GROUP_ROWS, HEAD_DIM)
    tot_f, tot_b = [], []
    for g in range(N_GROUPS):
        rows, qs, k_f, k_b, pin, sex, sin, pex, tf, tb = _group_scan(
            g, q_ref, zf_ref, zb_ref, lbf, lbb, (stq_ref, stk_ref))
        base_ref[BASE_QP, rows, :] = (qs * pin).reshape(flat)
        base_ref[BASE_KS, rows, :] = (k_f * sex).reshape(flat)
        base_ref[BASE_QS, rows, :] = (qs * sin).reshape(flat)
        base_ref[BASE_KP, rows, :] = (k_b * pex).reshape(flat)
        tot_f.append(tf)
        tot_b.append(tb)
    for si in range(1, len(HIGH_SPANS)):
        _level_factors(tot_f, tot_b, fac_ref, si)
    _chunk_factors(tot_f, tot_b, fac_ref, tf_ref, tb_ref)


def _chunk_bases(base_ref, r):
    return tuple(_chunk_rows(base_ref.at[k], r) for k in (BASE_QP, BASE_KS, BASE_QS, BASE_KP))


def _high_level(bases, fac_ref, si, r):
    qp, ks, qs, kp = bases
    span = HIGH_SPANS[si]
    ql = _by_group_parity(qs, qp, span)
    kl = _by_group_parity(ks, kp, span)
    if si > 0:
        ql = ql * _group_factor(fac_ref, 2 * (si - 1), r)
        kl = kl * _group_factor(fac_ref, 2 * (si - 1) + 1, r)
    return ql.astype(BF16), kl.astype(BF16)


def _nt_dot(a, b):
    return lax.dot_general(a, b, _NT, preferred_element_type=F32)


def _finish_chunk(r, a, bases, v_ref, o_ref, bwd_refs, st_ref, fac_ref, tf_ref):
    qp, ks, qs, kp = bases
    vb = _chunk_rows(v_ref, r).astype(BF16)
    st = st_ref[...]
    o = jnp.dot(a.astype(BF16), vb, preferred_element_type=F32)
    qf = (qp * _group_factor(fac_ref, FAC_QF, r)).astype(BF16)
    o = o + _nt_dot(qf, st.astype(BF16))
    kf = (ks * _group_factor(fac_ref, FAC_KF, r)).astype(BF16)
    upd = lax.dot_general(vb, kf, _TN, preferred_element_type=F32)
    st_ref[...] = st * tf_ref[pl.ds(r, 1), :] + upd
    o_ref[pl.ds(r, CHUNK, stride=CHUNKS_PER_BLOCK), :] = o
    rows = pl.ds(r * CHUNK, CHUNK)
    qb_ref, kb_ref, vb_ref = bwd_refs
    qb_ref[rows, :] = (qs * _group_factor(fac_ref, FAC_QB, r)).astype(BF16)
    kb_ref[rows, :] = (kp * _group_factor(fac_ref, FAC_KB, r)).astype(BF16)
    vb_ref[rows, :] = vb


def _matmul_ratio(v_ref, o_ref, bwd_refs, st_ref, stage, span):
    _, _, base_ref, fac_ref, tf_ref = stage
    row = lax.broadcasted_iota(jnp.int32, (CHUNK, CHUNK), 0)
    col = lax.broadcasted_iota(jnp.int32, (CHUNK, CHUNK), 1)
    whole_chunk = 2 * span == N_GROUPS
    far = (row ^ col) >= 2 * span * POS_BLOCK
    top = len(HIGH_SPANS) - 1
    for r in range(CHUNKS_PER_BLOCK):
        bases = _chunk_bases(base_ref, r)
        qp, ks, qs, kp = bases
        qi = _chunk_rows(base_ref.at[BASE_QI], r)
        ki = _chunk_rows(base_ref.at[BASE_KI], r)
        qf = _by_group_parity(qi, qp, span) * _group_factor(fac_ref, FAC_RQF, r)
        kf = _by_group_parity(ks, ki, span) * _group_factor(fac_ref, FAC_RKF, r)
        qb = _by_group_parity(qs, qi, span) * _group_factor(fac_ref, FAC_RQB, r)
        kb = _by_group_parity(ki, kp, span) * _group_factor(fac_ref, FAC_RKB, r)
        p_f = _nt_dot(qf.astype(BF16), kf.astype(BF16))
        p_b = _nt_dot(qb.astype(BF16), kb.astype(BF16))
        a = jnp.where(row >= col, p_f, p_b)
        if not whole_chunk:
            a = jnp.where(far, _nt_dot(*_high_level(bases, fac_ref, top, r)), a)
        a = a + jnp.where(row == col, p_b, 0.0)
        _finish_chunk(r, a, bases, v_ref, o_ref, bwd_refs, st_ref, fac_ref, tf_ref)


def _matmul_levels(v_ref, o_ref, bwd_refs, st_ref, stage):
    stq_ref, stk_ref, base_ref, fac_ref, tf_ref = stage
    row = lax.broadcasted_iota(jnp.int32, (CHUNK, CHUNK), 0)
    col = lax.broadcasted_iota(jnp.int32, (CHUNK, CHUNK), 1)
    xr = row ^ col
    lvl = jnp.zeros((CHUNK, CHUNK), jnp.int32)
    for b in range(N_LEVELS - 1):
        lvl = lvl + jnp.where(xr >= (1 << b), 1, 0)

    for r in range(CHUNKS_PER_BLOCK):
        bases = _chunk_bases(base_ref, r)
        a = None
        for li in range(N_LEVELS):
            if li < N_LOW:
                ql = _chunk_rows(stq_ref.at[li], r).astype(BF16)
                kl = _chunk_rows(stk_ref.at[li], r).astype(BF16)
            else:
                ql, kl = _high_level(bases, fac_ref, li - N_LOW, r)
            p = _nt_dot(ql, kl)
            a = p if a is None else jnp.where(lvl == li, p, a)
        _finish_chunk(r, a, bases, v_ref, o_ref, bwd_refs, st_ref, fac_ref, tf_ref)


def _rec_fwd_kernel(layer, lbf_ref, lbb_ref, q_ref, zf_ref, zb_ref, v_ref,
                    o_ref, qb_ref, kb_ref, vb_ref, tb_ref, *scratch):
    stage, st_ref = scratch[:-1], scratch[-1]
    bwd_refs = (qb_ref, kb_ref, vb_ref)
    span = RATIO_SPAN_FIRST if layer == 0 else RATIO_SPAN_LATER

    @pl.when(pl.program_id(1) == 0)
    def _():
        st_ref[...] = jnp.zeros_like(st_ref)

    smallest = _scan_ratio(layer, lbf_ref, lbb_ref, q_ref, zf_ref, zb_ref, tb_ref, stage, span)
    safe = smallest >= RATIO_MIN

    @pl.when(safe)
    def _():
        _matmul_ratio(v_ref, o_ref, bwd_refs, st_ref, stage, span)

    @pl.when(jnp.logical_not(safe))
    def _():
        _scan_levels(layer, lbf_ref, lbb_ref, q_ref, zf_ref, zb_ref, tb_ref, stage)
        _matmul_levels(v_ref, o_ref, bwd_refs, st_ref, stage)


def _group_index(group):
    if F32_GROUP0 <= group < F32_GROUP0 + N_F32_GROUPS:
        return group - F32_GROUP0
    return group if group < F32_GROUP0 else group - N_F32_GROUPS


def _rec_fwd(proj_f, proj_h, lb_fwd, lb_bwd, layer, d_hgrn):
    s = proj_f.shape[1]
    depth = lb_fwd.shape[0]
    nh = d_hgrn // HEAD_DIM
    nblk = s // TOK_BLOCK
    blk = (TOK_BLOCK, HEAD_DIM)
    hblk = (None,) + blk
    here = lambda h, i: (h, i, 0)

    def col(group):
        return pl.BlockSpec(hblk, lambda h, i, g=_group_index(group): (g * nh + h, i, 0))

    lb_spec = pl.BlockSpec((depth, HEAD_DIM), lambda h, i: (0, h))
    return pl.pallas_call(
        functools.partial(_rec_fwd_kernel, layer),
        grid=(nh, nblk),
        in_specs=[lb_spec, lb_spec, col(GROUP_Q), col(GROUP_ZF), col(GROUP_ZB), col(GROUP_V)],
        out_specs=[
            pl.BlockSpec(hblk, here),
            pl.BlockSpec(hblk, here),
            pl.BlockSpec(hblk, here),
            pl.BlockSpec(hblk, here),
            pl.BlockSpec((None, CHUNKS_PER_BLOCK, HEAD_DIM), here),
        ],
        out_shape=[jax.ShapeDtypeStruct((nh, s, HEAD_DIM), F32),
                   jax.ShapeDtypeStruct((nh, s, HEAD_DIM), BF16),
                   jax.ShapeDtypeStruct((nh, s, HEAD_DIM), BF16),
                   jax.ShapeDtypeStruct((nh, s, HEAD_DIM), BF16),
                   jax.ShapeDtypeStruct((nh, nblk * CHUNKS_PER_BLOCK, HEAD_DIM), F32)],
        scratch_shapes=[
            pltpu.VMEM((N_LOW,) + blk, F32),
            pltpu.VMEM((N_LOW,) + blk, F32),
            pltpu.VMEM((N_BASE,) + blk, F32),
            pltpu.VMEM((FAC_ROWS, HEAD_DIM), F32),
            pltpu.VMEM((CHUNKS_PER_BLOCK, HEAD_DIM), F32),
            pltpu.VMEM((HEAD_DIM, HEAD_DIM), F32),
        ],
        compiler_params=_params(("parallel", "arbitrary")),
        name="rec_fwd",
    )(lb_fwd, lb_bwd, proj_h, proj_f, proj_f, proj_f)


def _rec_bwd_kernel(qb_ref, kb_ref, vb_ref, tb_ref, op_ref, g_ref, nw_ref, y_ref, ob_ref, st_ref):
    @pl.when(pl.program_id(1) == 0)
    def _():
        st_ref[...] = jnp.zeros_like(st_ref)

    for blk in reversed(range(BWD_BLOCKS)):
        for r in reversed(range(CHUNKS_PER_BLOCK)):
            st = st_ref[...]
            rows = pl.ds(blk * TOK_BLOCK + r * CHUNK, CHUNK)
            strided = pl.ds(blk * TOK_BLOCK + r, CHUNK, stride=CHUNKS_PER_BLOCK)
            ob_ref[strided, :] = _nt_dot(qb_ref[rows, :], st.astype(BF16))
            upd = lax.dot_general(vb_ref[rows, :], kb_ref[rows, :], _TN,
                                  preferred_element_type=F32)
            st_ref[...] = st * tb_ref[pl.ds(blk * CHUNKS_PER_BLOCK + r, 1), :] + upd
    g = g_ref[...].astype(F32)
    y = _rms(op_ref[...] + ob_ref[...], nw_ref[...]) * (g * _sigmoid(g))
    y_ref[...] = y.astype(y_ref.dtype)


def _rec_bwd(qb, kb, vb, tb, proj_h, o_part, norm_w, d_hgrn):
    s = proj_h.shape[1]
    nh = d_hgrn // HEAD_DIM
    rows = BWD_BLOCKS * TOK_BLOCK
    nstep = s // rows
    blk = (rows, HEAD_DIM)
    hblk = (None,) + blk
    rev = lambda h, i: (h, nstep - 1 - i, 0)

    def col(group):
        return pl.BlockSpec(hblk, lambda h, i, g=_group_index(group): (g * nh + h, nstep - 1 - i, 0))

    return pl.pallas_call(
        _rec_bwd_kernel,
        grid=(nh, nstep),
        in_specs=[
            pl.BlockSpec(hblk, rev),
            pl.BlockSpec(hblk, rev),
            pl.BlockSpec(hblk, rev),
            pl.BlockSpec((None, BWD_BLOCKS * CHUNKS_PER_BLOCK, HEAD_DIM), rev),
            pl.BlockSpec(hblk, rev),
            col(GROUP_GATE),
            pl.BlockSpec((1, HEAD_DIM), lambda h, i: (0, h)),
        ],
        out_specs=pl.BlockSpec(blk, lambda h, i: (nstep - 1 - i, h)),
        out_shape=jax.ShapeDtypeStruct((s, d_hgrn), BF16),
        scratch_shapes=[pltpu.VMEM(blk, F32), pltpu.VMEM((HEAD_DIM, HEAD_DIM), F32)],
        compiler_params=_params(("parallel", "arbitrary")),
        name="rec_bwd",
    )(qb, kb, vb, tb, o_part, proj_h, norm_w)


def _conv_kernel(b_ref, c_ref, h_ref, cp_ref, hp_ref, cn_ref, hn_ref, w_ref, y_ref):
    i = pl.program_id(0)
    last = pl.num_programs(0) - 1
    n = c_ref.shape[1]
    sub = lax.broadcasted_iota(jnp.int32, (SUBLANES, LANES), 0)
    for c in range(c_ref.shape[0]):
        u = c_ref[c].astype(F32) * h_ref[c].astype(F32)
        halo_p = (cp_ref[c].astype(F32) * hp_ref[c].astype(F32))[HALO_ROWS - SUBLANES:]
        halo_p = jnp.where(i > 0, halo_p, 0.0)
        first = jnp.where(sub == 0, pltpu.roll(halo_p, 1, 0), pltpu.roll(u[n - SUBLANES:], 1, 0))
        u_prev = jnp.concatenate([first, u[:n - SUBLANES]], axis=0)
        halo_n = (cn_ref[c].astype(F32) * hn_ref[c].astype(F32))[:SUBLANES]
        halo_n = jnp.where(i < last, halo_n, 0.0)
        lastv = jnp.where(sub == SUBLANES - 1, pltpu.roll(halo_n, SUBLANES - 1, 0),
                          pltpu.roll(u[:SUBLANES], SUBLANES - 1, 0))
        u_next = jnp.concatenate([u[SUBLANES:], lastv], axis=0)
        w = w_ref[:, c * LANES:(c + 1) * LANES]
        y = w[0:1] * u_prev + w[1:2] * u + w[2:3] * u_next
        y_ref[:, c * LANES:(c + 1) * LANES] = (b_ref[c].astype(F32) * y).astype(y_ref.dtype)


def _conv(proj_h, conv_w, d_hgrn, d_conv, *, tc):
    s = proj_h.shape[1]
    nblk = s // TOK_BLOCK
    nhalo = s // HALO_ROWS
    per_blk = TOK_BLOCK // HALO_ROWS
    slabs = tc // LANES
    ncol = d_conv // tc

    def first_block(group):
        return _group_index(group) * (d_hgrn // tc)

    def main(group):
        return pl.BlockSpec((slabs, TOK_BLOCK, LANES),
                            lambda i, j, b=first_block(group): (b + j, i, 0))

    def prev(group):
        return pl.BlockSpec((slabs, HALO_ROWS, LANES),
                            lambda i, j, b=first_block(group): (b + j,
                                                                jnp.maximum(i * per_blk - 1, 0), 0))

    def nxt(group):
        return pl.BlockSpec((slabs, HALO_ROWS, LANES),
                            lambda i, j, b=first_block(group): (
                                b + j, jnp.minimum((i + 1) * per_blk, nhalo - 1), 0))

    return pl.pallas_call(
        _conv_kernel,
        grid=(nblk, ncol),
        in_specs=[main(GROUP_CB), main(GROUP_CC), main(GROUP_CH), prev(GROUP_CC), prev(GROUP_CH),
                  nxt(GROUP_CC), nxt(GROUP_CH),
                  pl.BlockSpec((CONV_WIDTH, tc), lambda i, j: (0, j))],
        out_specs=pl.BlockSpec((TOK_BLOCK, tc), lambda i, j: (i, j)),
        out_shape=jax.ShapeDtypeStruct((s, d_conv), BF16),
        compiler_params=_params(("parallel", "parallel")),
        name="short_conv",
    )(proj_h, proj_h, proj_h, proj_h, proj_h, proj_h, proj_h, conv_w)


def _out_proj_kernel(x_ref, yr_ref, yc_ref, wr_ref, wc_ref, o_ref):
    acc = jnp.dot(yr_ref[...], wr_ref[...], preferred_element_type=F32)
    acc = acc + jnp.dot(yc_ref[...], wc_ref[...], preferred_element_type=F32)
    o_ref[...] = x_ref[...] + acc


def _out_proj(x, y_rec, y_conv, w, layer, *, tm):
    s, d = x.shape
    d_rec = y_rec.shape[1]
    d_conv = y_conv.shape[1]
    assert d_rec == d_conv
    return pl.pallas_call(
        _out_proj_kernel,
        grid=(s // tm,),
        in_specs=[
            pl.BlockSpec((tm, d), lambda i: (i, 0)),
            pl.BlockSpec((tm, d_rec), lambda i: (i, 0)),
            pl.BlockSpec((tm, d_conv), lambda i: (i, 0)),
            pl.BlockSpec((None, d_rec, d), lambda i: (layer, 0, 0)),
            pl.BlockSpec((None, d_conv, d), lambda i: (layer, 1, 0)),
        ],
        out_specs=pl.BlockSpec((tm, d), lambda i: (i, 0)),
        out_shape=jax.ShapeDtypeStruct((s, d), F32),
        compiler_params=_params(("parallel",)),
        name="out_proj",
    )(x, y_rec, y_conv, w, w)


def _ffn_kernel(final, x_ref, nw_ref, wg_ref, wu_ref, wd_ref, fw_ref, o_ref, h_scr):
    f = pl.program_id(1)

    def tile(h):
        g = jnp.dot(h, wg_ref[...], preferred_element_type=F32)
        u = jnp.dot(h, wu_ref[...], preferred_element_type=F32)
        a = (g * _sigmoid(g) * u).astype(BF16)
        return jnp.dot(a, wd_ref[...], preferred_element_type=F32)

    @pl.when(f == 0)
    def _():
        rows = x_ref.shape[0] // NORM_SPLIT
        for m in range(NORM_SPLIT):
            sl = pl.ds(m * rows, rows)
            x = x_ref[sl, :]
            h = _rms(x, nw_ref[...]).astype(BF16)
            h_scr[sl, :] = h
            o_ref[sl, :] = x + tile(h)

    @pl.when(f > 0)
    def _():
        o_ref[...] += tile(h_scr[...])

    if final:
        @pl.when(f == pl.num_programs(1) - 1)
        def _():
            o_ref[...] = _rms(o_ref[...], fw_ref[...])


def _ffn(x, nw, w_gu, w_d, fw, layer, final, *, tm, tf):
    s, d = x.shape
    d_ff = w_d.shape[1]
    nf = d_ff // tf
    return pl.pallas_call(
        functools.partial(_ffn_kernel, final),
        grid=(s // tm, nf),
        in_specs=[
            pl.BlockSpec((tm, d), lambda i, f: (i, 0)),
            pl.BlockSpec((1, d), lambda i, f: (0, 0)),
            pl.BlockSpec((None, d, tf), lambda i, f: (layer, 0, f)),
            pl.BlockSpec((None, d, tf), lambda i, f: (layer, 0, nf + f)),
            pl.BlockSpec((None, tf, d), lambda i, f: (layer, f, 0)),
            pl.BlockSpec((1, d), lambda i, f: (0, 0)),
        ],
        out_specs=pl.BlockSpec((tm, d), lambda i, f: (i, 0)),
        out_shape=jax.ShapeDtypeStruct((s, d), F32),
        scratch_shapes=[pltpu.VMEM((tm, d), BF16)],
        compiler_params=_params(("parallel", "arbitrary")),
        name="ffn",
    )(x, nw, w_gu, w_gu, w_d, fw)


def _interleave(x):
    s, d = x.shape
    x4 = x.reshape(s // TOK_BLOCK, CHUNKS_PER_BLOCK, CHUNK, d)
    return x4.transpose(0, 2, 1, 3).reshape(s, d)


def _deinterleave(x):
    s, d = x.shape
    x4 = x.reshape(s // TOK_BLOCK, CHUNK, CHUNKS_PER_BLOCK, d)
    return x4.transpose(0, 2, 1, 3).reshape(s, d)


def kernel(x, attn_norm_w, w_in, lb_fwd, lb_bwd, hgrn_norm_w, conv_w, w_out, ffn_norm_w,
           w_gate_up, w_down, final_norm_w):
    bsz, s, d = x.shape
    depth = w_in.shape[0]
    d_hgrn = lb_fwd.shape[1]
    d_conv = conv_w.shape[2]
    assert bsz == 1 and s % (BWD_BLOCKS * TOK_BLOCK) == 0
    assert w_in.shape[2] == 5 * d_hgrn + 3 * d_conv and d_hgrn == d_conv

    xs = _interleave(x[0])
    w_in, w_out, w_gate_up, w_down = (_cast_bf16(w) for w in (w_in, w_out, w_gate_up, w_down))
    fw = final_norm_w.reshape(1, d)
    for l in range(depth):
        proj_f, proj_h = _in_proj(xs, attn_norm_w[l].reshape(1, d), w_in, l, tm=1024, tn=d_hgrn)
        o_part, qb, kb, vb, tb = _rec_fwd(proj_f, proj_h, lb_fwd, lb_bwd, l, d_hgrn)
        y_rec = _rec_bwd(qb, kb, vb, tb, proj_h, o_part, hgrn_norm_w[l].reshape(1, d_hgrn), d_hgrn)
        y_conv = _conv(proj_h, conv_w[l], d_hgrn, d_conv, tc=512)
        xs = _out_proj(xs, y_rec, y_conv, w_out, l, tm=512)
        xs = _ffn(xs, ffn_norm_w[l].reshape(1, d), w_gate_up, w_down, fw, l, l == depth - 1,
                  tm=1024, tf=512)
    return _deinterleave(xs)[None]
```

```python
import functools

import jax
import jax.numpy as jnp
from jax import lax
from jax.experimental import pallas as pl
from jax.experimental.pallas import tpu as pltpu

F32 = jnp.float32
BF16 = jnp.bfloat16

EPS = 1e-6
HEAD_DIM = 128
CHUNK = 128
SUBLANES = 8
LANES = 128
CHUNKS_PER_BLOCK = SUBLANES
TOK_BLOCK = CHUNK * CHUNKS_PER_BLOCK
POS_BLOCK = SUBLANES
N_GROUPS = CHUNK // POS_BLOCK
GROUP_ROWS = POS_BLOCK * CHUNKS_PER_BLOCK
LOW_LEVELS = tuple(1 << b for b in range(POS_BLOCK.bit_length() - 1))
HIGH_SPANS = tuple(1 << b for b in range(N_GROUPS.bit_length() - 1))
N_LOW = len(LOW_LEVELS) + 1
N_LEVELS = N_LOW + len(HIGH_SPANS)
N_FAC_LEVEL = 2 * (len(HIGH_SPANS) - 1)
FAC_QF, FAC_KF, FAC_QB, FAC_KB = (N_FAC_LEVEL + i for i in range(4))
FAC_RQF, FAC_RKF, FAC_RQB, FAC_RKB = (N_FAC_LEVEL + 4 + i for i in range(4))
N_FAC = N_FAC_LEVEL + 8
FAC_ROWS = N_FAC * N_GROUPS * CHUNKS_PER_BLOCK
BASE_QP, BASE_KS, BASE_QS, BASE_KP, BASE_QI, BASE_KI = range(6)
N_BASE = 6
RATIO_SPAN_FIRST = 4
RATIO_SPAN_LATER = N_GROUPS // 2
RATIO_MIN = 2.0 ** -100
BWD_BLOCKS = 8
GROUP_Q, GROUP_ZF, GROUP_ZB, GROUP_V, GROUP_GATE, GROUP_CB, GROUP_CC, GROUP_CH = range(8)
F32_GROUP0, N_F32_GROUPS = GROUP_ZF, 3
NORM_SPLIT = 4
CONV_WIDTH = 3
HALO_ROWS = 2 * SUBLANES
VMEM_LIMIT = 58 * 1024 * 1024
CAST_BLOCK_BYTES = 12 * 1024 * 1024

_NT = (((1,), (1,)), ((), ()))
_TN = (((0,), (0,)), ((), ()))


def _sigmoid(x):
    return 0.5 * jnp.tanh(0.5 * x) + 0.5


def _rms(x, w):
    ms = jnp.mean(x * x, axis=-1, keepdims=True)
    return x * lax.rsqrt(ms + EPS) * w


def _params(sem):
    return pltpu.CompilerParams(dimension_semantics=sem, vmem_limit_bytes=VMEM_LIMIT)


def _cast_kernel(w_ref, o_ref):
    o_ref[...] = w_ref[...].astype(o_ref.dtype)


def _cast_bf16(w):
    depth, r, c = w.shape
    rows = depth * r
    tr = 1 << ((CAST_BLOCK_BYTES // (4 * c)).bit_length() - 1)
    assert tr >= 2 * SUBLANES and rows % tr == 0
    out = pl.pallas_call(
        _cast_kernel,
        grid=(rows // tr,),
        in_specs=[pl.BlockSpec((tr, c), lambda i: (i, 0))],
        out_specs=pl.BlockSpec((tr, c), lambda i: (i, 0)),
        out_shape=jax.ShapeDtypeStruct((rows, c), BF16),
        compiler_params=_params(("parallel",)),
        name="cast_bf16",
    )(w.reshape(rows, c))
    return out.reshape(depth, r, c)


def _in_proj_kernel(x_ref, nw_ref, w_ref, of_ref, oh_ref, h_scr):
    j = pl.program_id(1)

    def project(o_ref, h, sl):
        res = jnp.dot(h, w_ref[...], preferred_element_type=F32)
        for c in range(o_ref.shape[0]):
            o_ref[c, sl, :] = res[:, c * LANES:(c + 1) * LANES].astype(o_ref.dtype)

    @pl.when(j == 0)
    def _():
        rows = x_ref.shape[0] // NORM_SPLIT
        for m in range(NORM_SPLIT):
            sl = pl.ds(m * rows, rows)
            h = _rms(x_ref[sl, :], nw_ref[...]).astype(BF16)
            h_scr[sl, :] = h
            project(oh_ref, h, sl)

    wide = jnp.logical_and(j >= F32_GROUP0, j < F32_GROUP0 + N_F32_GROUPS)
    narrow = jnp.logical_and(j > 0, jnp.logical_not(wide))
    pl.when(wide)(lambda: project(of_ref, h_scr[...], slice(None)))
    pl.when(narrow)(lambda: project(oh_ref, h_scr[...], slice(None)))


def _in_proj(x, nw, w, layer, *, tm, tn):
    s, d = x.shape
    n = w.shape[2]
    ngroups = n // tn
    slabs = tn // LANES
    assert F32_GROUP0 == 1 and ngroups > F32_GROUP0 + N_F32_GROUPS
    nhalf = ngroups - N_F32_GROUPS
    return pl.pallas_call(
        _in_proj_kernel,
        grid=(s // tm, ngroups),
        in_specs=[
            pl.BlockSpec((tm, d), lambda i, j: (i, 0)),
            pl.BlockSpec((1, d), lambda i, j: (0, 0)),
            pl.BlockSpec((None, d, tn), lambda i, j: (layer, 0, j)),
        ],
        out_specs=[
            pl.BlockSpec((slabs, tm, LANES),
                         lambda i, j: (jnp.clip(j - F32_GROUP0, 0, N_F32_GROUPS - 1), i, 0)),
            pl.BlockSpec((slabs, tm, LANES),
                         lambda i, j: (jnp.clip(j - N_F32_GROUPS, 0, nhalf - 1), i, 0)),
        ],
        out_shape=[jax.ShapeDtypeStruct((N_F32_GROUPS * slabs, s, LANES), F32),
                   jax.ShapeDtypeStruct((nhalf * slabs, s, LANES), BF16)],
        scratch_shapes=[pltpu.VMEM((tm, d), BF16)],
        compiler_params=_params(("parallel", "arbitrary")),
        name="in_proj",
    )(x, nw, w)


def _lower_bound(lb_ref, layer):
    p = lb_ref[...]
    if layer == 0:
        return jnp.zeros((1, p.shape[1]), F32)
    e = jnp.exp(p - jnp.max(p, axis=0, keepdims=True))
    sm = e / jnp.sum(e, axis=0, keepdims=True)
    return jnp.sum(sm[1:layer + 1], axis=0, keepdims=True)


def _gates(z, lb):
    th = 0.5 * jnp.tanh(0.5 * z)
    half = 0.5 * (1.0 - lb)
    f = (lb + half) + (1.0 - lb) * th
    k = half - (1.0 - lb) * th
    return f, k


def _halves(x, h):
    x5 = x.reshape((x.shape[0] // (2 * h), 2, h) + x.shape[1:])
    return x5[:, 0], x5[:, 1]


def _join(a, b):
    x = jnp.stack([a, b], axis=1)
    return x.reshape((x.shape[0] * x.shape[1] * x.shape[2],) + a.shape[2:])


def _chunk_rows(ref, r):
    return ref[pl.ds(r, CHUNK, stride=CHUNKS_PER_BLOCK), :]


def _fac_rows(k, g):
    return pl.ds((k * N_GROUPS + g) * CHUNKS_PER_BLOCK, CHUNKS_PER_BLOCK)


def _group_factor(fac_ref, k, r):
    return jnp.concatenate(
        [jnp.broadcast_to(fac_ref[pl.ds((k * N_GROUPS + g) * CHUNKS_PER_BLOCK + r, 1), :],
                          (POS_BLOCK, HEAD_DIM))
         for g in range(N_GROUPS)], axis=0)


def _by_group_parity(even, odd, span):
    pieces = []
    for g in range(N_GROUPS):
        src = odd if (g // span) % 2 else even
        pieces.append(src[g * POS_BLOCK:(g + 1) * POS_BLOCK])
    return jnp.concatenate(pieces, axis=0)


def _seg_products(tot, span):
    ones = jnp.ones_like(tot[0])
    pre, suf = [None] * N_GROUPS, [None] * N_GROUPS
    for s0 in range(0, N_GROUPS, span):
        acc = None
        for g in range(s0, s0 + span):
            pre[g] = ones if acc is None else acc
            acc = tot[g] if acc is None else acc * tot[g]
        acc = None
        for g in reversed(range(s0, s0 + span)):
            suf[g] = ones if acc is None else acc
            acc = tot[g] if acc is None else acc * tot[g]
    return pre, suf


def _group_scan(g, q_ref, zf_ref, zb_ref, lbf, lbb, low_stage=None):
    sh = (POS_BLOCK, CHUNKS_PER_BLOCK, HEAD_DIM)
    flat = (GROUP_ROWS, HEAD_DIM)
    rows = pl.ds(g * GROUP_ROWS, GROUP_ROWS)
    q = q_ref[rows, :].astype(F32).reshape(sh)
    qs = q * _sigmoid(q)
    f_f, k_f = _gates(zf_ref[rows, :].reshape(sh), lbf)
    f_b, k_b = _gates(zb_ref[rows, :].reshape(sh), lbb)
    if low_stage is not None:
        stq_ref, stk_ref = low_stage
        stq_ref[0, rows, :] = qs.reshape(flat)
        stk_ref[0, rows, :] = (k_f + k_b).reshape(flat)
    pin, sex, tf = f_f, jnp.ones(sh, F32), f_f
    sin, pex, tb = f_b, jnp.ones(sh, F32), f_b
    for li, h in enumerate(LOW_LEVELS):
        pin1, pin2 = _halves(pin, h)
        sex1, sex2 = _halves(sex, h)
        sin1, sin2 = _halves(sin, h)
        pex1, pex2 = _halves(pex, h)
        if low_stage is not None:
            qs1, qs2 = _halves(qs, h)
            kf1, _ = _halves(k_f, h)
            _, kb2 = _halves(k_b, h)
            stq_ref[li + 1, rows, :] = _join(qs1 * sin1, qs2 * pin2).reshape(flat)
            stk_ref[li + 1, rows, :] = _join(kf1 * sex1, kb2 * pex2).reshape(flat)
        tf1, tf2 = _halves(tf, 1)
        tb1, tb2 = _halves(tb, 1)
        pin = _join(pin1, pin2 * tf1)
        sex = _join(sex1 * tf2, sex2)
        sin = _join(sin1 * tb2, sin2)
        pex = _join(pex1, pex2 * tb1)
        tf = (tf1 * tf2).reshape((tf.shape[0] // 2,) + sh[1:])
        tb = (tb1 * tb2).reshape((tb.shape[0] // 2,) + sh[1:])
    return rows, qs, k_f, k_b, pin, sex, sin, pex, tf[0], tb[0]


def _chunk_factors(tot_f, tot_b, fac_ref, tf_ref, tb_ref):
    pre_f, suf_f = _seg_products(tot_f, N_GROUPS)
    pre_b, suf_b = _seg_products(tot_b, N_GROUPS)
    for g in range(N_GROUPS):
        fac_ref[_fac_rows(FAC_QF, g), :] = pre_f[g]
        fac_ref[_fac_rows(FAC_KF, g), :] = suf_f[g]
        fac_ref[_fac_rows(FAC_QB, g), :] = suf_b[g]
        fac_ref[_fac_rows(FAC_KB, g), :] = pre_b[g]
    tf_ref[...] = pre_f[N_GROUPS - 1] * tot_f[N_GROUPS - 1]
    tb_ref[...] = pre_b[N_GROUPS - 1] * tot_b[N_GROUPS - 1]


def _level_factors(tot_f, tot_b, fac_ref, si):
    span = HIGH_SPANS[si]
    pre_f, suf_f = _seg_products(tot_f, span)
    pre_b, suf_b = _seg_products(tot_b, span)
    for g in range(N_GROUPS):
        second = (g // span) % 2
        fac_ref[_fac_rows(2 * (si - 1), g), :] = pre_f[g] if second else suf_b[g]
        fac_ref[_fac_rows(2 * (si - 1) + 1, g), :] = pre_b[g] if second else suf_f[g]


def _scan_ratio(layer, lbf_ref, lbb_ref, q_ref, zf_ref, zb_ref, tb_ref, stage, span):
    _, _, base_ref, fac_ref, tf_ref = stage
    lbf = _lower_bound(lbf_ref, layer)
    lbb = _lower_bound(lbb_ref, layer)
    flat = (GROUP_ROWS, HEAD_DIM)
    tot_f, tot_b = [], []
    for g in range(N_GROUPS):
        rows, qs, k_f, k_b, pin, sex, sin, pex, tf, tb = _group_scan(
            g, q_ref, zf_ref, zb_ref, lbf, lbb)
        base_ref[BASE_QP, rows, :] = (qs * pin).reshape(flat)
        base_ref[BASE_KS, rows, :] = (k_f * sex).reshape(flat)
        base_ref[BASE_QS, rows, :] = (qs * sin).reshape(flat)
        base_ref[BASE_KP, rows, :] = (k_b * pex).reshape(flat)
        if (g // span) % 2:
            base_ref[BASE_QI, rows, :] = (qs * (1.0 / pex)).reshape(flat)
            base_ref[BASE_KI, rows, :] = (k_f * (1.0 / pin)).reshape(flat)
        else:
            base_ref[BASE_QI, rows, :] = (qs * (1.0 / sex)).reshape(flat)
            base_ref[BASE_KI, rows, :] = (k_b * (1.0 / sin)).reshape(flat)
        tot_f.append(tf)
        tot_b.append(tb)

    pre_f, suf_f = _seg_products(tot_f, span)
    pre_b, suf_b = _seg_products(tot_b, span)
    smallest = None
    for g in range(N_GROUPS):
        second = (g // span) % 2
        fac_ref[_fac_rows(FAC_RQF, g), :] = pre_f[g] if second else 1.0 / suf_f[g]
        fac_ref[_fac_rows(FAC_RKF, g), :] = 1.0 / pre_f[g] if second else suf_f[g]
        fac_ref[_fac_rows(FAC_RQB, g), :] = 1.0 / pre_b[g] if second else suf_b[g]
        fac_ref[_fac_rows(FAC_RKB, g), :] = pre_b[g] if second else 1.0 / suf_b[g]
        if g % span == span - 1:
            run = jnp.minimum(pre_f[g] * tot_f[g], pre_b[g] * tot_b[g])
            smallest = run if smallest is None else jnp.minimum(smallest, run)
    if 2 * span < N_GROUPS:
        _level_factors(tot_f, tot_b, fac_ref, len(HIGH_SPANS) - 1)
    _chunk_factors(tot_f, tot_b, fac_ref, tf_ref, tb_ref)
    return jnp.min(smallest)


def _scan_levels(layer, lbf_ref, lbb_ref, q_ref, zf_ref, zb_ref, tb_ref, stage):
    stq_ref, stk_ref, base_ref, fac_ref, tf_ref = stage
    lbf = _lower_bound(lbf_ref, layer)
    lbb = _lower_bound(lbb_ref, layer)
    flat = (GROUP_ROWS, HEAD_DIM)
    tot_f, tot_b = [], []
    for g in range(N_GROUPS):
        rows, qs, k_f, k_b, pin, sex, sin, pex, tf, tb = _group_scan(
            g, q_ref, zf_ref, zb_ref, lbf, lbb, (stq_ref, stk_ref))
        base_ref[BASE_QP, rows, :] = (qs * pin).reshape(flat)
        base_ref[BASE_KS, rows, :] = (k_f * sex).reshape(flat)
        base_ref[BASE_QS, rows, :] = (qs * sin).reshape(flat)
        base_ref[BASE_KP, rows, :] = (k_b * pex).reshape(flat)
        tot_f.append(tf)
        tot_b.append(tb)
    for si in range(1, len(HIGH_SPANS)):
        _level_factors(tot_f, tot_b, fac_ref, si)
    _chunk_factors(tot_f, tot_b, fac_ref, tf_ref, tb_ref)


def _chunk_bases(base_ref, r):
    return tuple(_chunk_rows(base_ref.at[k], r) for k in (BASE_QP, BASE_KS, BASE_QS, BASE_KP))


def _high_level(bases, fac_ref, si, r):
    qp, ks, qs, kp = bases
    span = HIGH_SPANS[si]
    ql = _by_group_parity(qs, qp, span)
    kl = _by_group_parity(ks, kp, span)
    if si > 0:
        ql = ql * _group_factor(fac_ref, 2 * (si - 1), r)
        kl = kl * _group_factor(fac_ref, 2 * (si - 1) + 1, r)
    return ql.astype(BF16), kl.astype(BF16)


def _nt_dot(a, b):
    return lax.dot_general(a, b, _NT, preferred_element_type=F32)


def _finish_chunk(r, a, bases, v_ref, o_ref, bwd_refs, st_ref, fac_ref, tf_ref):
    qp, ks, qs, kp = bases
    vb = _chunk_rows(v_ref, r).astype(BF16)
    st = st_ref[...]
    o = jnp.dot(a.astype(BF16), vb, preferred_element_type=F32)
    qf = (qp * _group_factor(fac_ref, FAC_QF, r)).astype(BF16)
    o = o + _nt_dot(qf, st.astype(BF16))
    kf = (ks * _group_factor(fac_ref, FAC_KF, r)).astype(BF16)
    upd = lax.dot_general(vb, kf, _TN, preferred_element_type=F32)
    st_ref[...] = st * tf_ref[pl.ds(r, 1), :] + upd
    o_ref[pl.ds(r, CHUNK, stride=CHUNKS_PER_BLOCK), :] = o
    rows = pl.ds(r * CHUNK, CHUNK)
    qb_ref, kb_ref, vb_ref = bwd_refs
    qb_ref[rows, :] = (qs * _group_factor(fac_ref, FAC_QB, r)).astype(BF16)
    kb_ref[rows, :] = (kp * _group_factor(fac_ref, FAC_KB, r)).astype(BF16)
    vb_ref[rows, :] = vb


def _matmul_ratio(v_ref, o_ref, bwd_refs, st_ref, stage, span):
    _, _, base_ref, fac_ref, tf_ref = stage
    row = lax.broadcasted_iota(jnp.int32, (CHUNK, CHUNK), 0)
    col = lax.broadcasted_iota(jnp.int32, (CHUNK, CHUNK), 1)
    whole_chunk = 2 * span == N_GROUPS
    far = (row ^ col) >= 2 * span * POS_BLOCK
    top = len(HIGH_SPANS) - 1
    for r in range(CHUNKS_PER_BLOCK):
        bases = _chunk_bases(base_ref, r)
        qp, ks, qs, kp = bases
        qi = _chunk_rows(base_ref.at[BASE_QI], r)
        ki = _chunk_rows(base_ref.at[BASE_KI], r)
        qf = _by_group_parity(qi, qp, span) * _group_factor(fac_ref, FAC_RQF, r)
        kf = _by_group_parity(ks, ki, span) * _group_factor(fac_ref, FAC_RKF, r)
        qb = _by_group_parity(qs, qi, span) * _group_factor(fac_ref, FAC_RQB, r)
        kb = _by_group_parity(ki, kp, span) * _group_factor(fac_ref, FAC_RKB, r)
        p_f = _nt_dot(qf.astype(BF16), kf.astype(BF16))
        p_b = _nt_dot(qb.astype(BF16), kb.astype(BF16))
        a = jnp.where(row >= col, p_f, p_b)
        if not whole_chunk:
            a = jnp.where(far, _nt_dot(*_high_level(bases, fac_ref, top, r)), a)
        a = a + jnp.where(row == col, p_b, 0.0)
        _finish_chunk(r, a, bases, v_ref, o_ref, bwd_refs, st_ref, fac_ref, tf_ref)


def _matmul_levels(v_ref, o_ref, bwd_refs, st_ref, stage):
    stq_ref, stk_ref, base_ref, fac_ref, tf_ref = stage
    row = lax.broadcasted_iota(jnp.int32, (CHUNK, CHUNK), 0)
    col = lax.broadcasted_iota(jnp.int32, (CHUNK, CHUNK), 1)
    xr = row ^ col
    lvl = jnp.zeros((CHUNK, CHUNK), jnp.int32)
    for b in range(N_LEVELS - 1):
        lvl = lvl + jnp.where(xr >= (1 << b), 1, 0)

    for r in range(CHUNKS_PER_BLOCK):
        bases = _chunk_bases(base_ref, r)
        a = None
        for li in range(N_LEVELS):
            if li < N_LOW:
                ql = _chunk_rows(stq_ref.at[li], r).astype(BF16)
                kl = _chunk_rows(stk_ref.at[li], r).astype(BF16)
            else:
                ql, kl = _high_level(bases, fac_ref, li - N_LOW, r)
            p = _nt_dot(ql, kl)
            a = p if a is None else jnp.where(lvl == li, p, a)
        _finish_chunk(r, a, bases, v_ref, o_ref, bwd_refs, st_ref, fac_ref, tf_ref)


def _rec_fwd_kernel(layer, lbf_ref, lbb_ref, q_ref, zf_ref, zb_ref, v_ref,
                    o_ref, qb_ref, kb_ref, vb_ref, tb_ref, *scratch):
    stage, st_ref = scratch[:-1], scratch[-1]
    bwd_refs = (qb_ref, kb_ref, vb_ref)
    span = RATIO_SPAN_FIRST if layer == 0 else RATIO_SPAN_LATER

    @pl.when(pl.program_id(1) == 0)
    def _():
        st_ref[...] = jnp.zeros_like(st_ref)

    smallest = _scan_ratio(layer, lbf_ref, lbb_ref, q_ref, zf_ref, zb_ref, tb_ref, stage, span)
    safe = smallest >= RATIO_MIN

    @pl.when(safe)
    def _():
        _matmul_ratio(v_ref, o_ref, bwd_refs, st_ref, stage, span)

    @pl.when(jnp.logical_not(safe))
    def _():
        _scan_levels(layer, lbf_ref, lbb_ref, q_ref, zf_ref, zb_ref, tb_ref, stage)
        _matmul_levels(v_ref, o_ref, bwd_refs, st_ref, stage)


def _group_index(group):
    if F32_GROUP0 <= group < F32_GROUP0 + N_F32_GROUPS:
        return group - F32_GROUP0
    return group if group < F32_GROUP0 else group - N_F32_GROUPS


def _rec_fwd(proj_f, proj_h, lb_fwd, lb_bwd, layer, d_hgrn):
    s = proj_f.shape[1]
    depth = lb_fwd.shape[0]
    nh = d_hgrn // HEAD_DIM
    nblk = s // TOK_BLOCK
    blk = (TOK_BLOCK, HEAD_DIM)
    hblk = (None,) + blk
    here = lambda h, i: (h, i, 0)

    def col(group):
        return pl.BlockSpec(hblk, lambda h, i, g=_group_index(group): (g * nh + h, i, 0))

    lb_spec = pl.BlockSpec((depth, HEAD_DIM), lambda h, i: (0, h))
    return pl.pallas_call(
        functools.partial(_rec_fwd_kernel, layer),
        grid=(nh, nblk),
        in_specs=[lb_spec, lb_spec, col(GROUP_Q), col(GROUP_ZF), col(GROUP_ZB), col(GROUP_V)],
        out_specs=[
            pl.BlockSpec(hblk, here),
            pl.BlockSpec(hblk, here),
            pl.BlockSpec(hblk, here),
            pl.BlockSpec(hblk, here),
            pl.BlockSpec((None, CHUNKS_PER_BLOCK, HEAD_DIM), here),
        ],
        out_shape=[jax.ShapeDtypeStruct((nh, s, HEAD_DIM), F32),
                   jax.ShapeDtypeStruct((nh, s, HEAD_DIM), BF16),
                   jax.ShapeDtypeStruct((nh, s, HEAD_DIM), BF16),
                   jax.ShapeDtypeStruct((nh, s, HEAD_DIM), BF16),
                   jax.ShapeDtypeStruct((nh, nblk * CHUNKS_PER_BLOCK, HEAD_DIM), F32)],
        scratch_shapes=[
            pltpu.VMEM((N_LOW,) + blk, F32),
            pltpu.VMEM((N_LOW,) + blk, F32),
            pltpu.VMEM((N_BASE,) + blk, F32),
            pltpu.VMEM((FAC_ROWS, HEAD_DIM), F32),
            pltpu.VMEM((CHUNKS_PER_BLOCK, HEAD_DIM), F32),
            pltpu.VMEM((HEAD_DIM, HEAD_DIM), F32),
        ],
        compiler_params=_params(("parallel", "arbitrary")),
        name="rec_fwd",
    )(lb_fwd, lb_bwd, proj_h, proj_f, proj_f, proj_f)


def _rec_bwd_kernel(qb_ref, kb_ref, vb_ref, tb_ref, op_ref, g_ref, nw_ref, y_ref, ob_ref, st_ref):
    @pl.when(pl.program_id(1) == 0)
    def _():
        st_ref[...] = jnp.zeros_like(st_ref)

    for blk in reversed(range(BWD_BLOCKS)):
        for r in reversed(range(CHUNKS_PER_BLOCK)):
            st = st_ref[...]
            rows = pl.ds(blk * TOK_BLOCK + r * CHUNK, CHUNK)
            strided = pl.ds(blk * TOK_BLOCK + r, CHUNK, stride=CHUNKS_PER_BLOCK)
            ob_ref[strided, :] = _nt_dot(qb_ref[rows, :], st.astype(BF16))
            upd = lax.dot_general(vb_ref[rows, :], kb_ref[rows, :], _TN,
                                  preferred_element_type=F32)
            st_ref[...] = st * tb_ref[pl.ds(blk * CHUNKS_PER_BLOCK + r, 1), :] + upd
    g = g_ref[...].astype(F32)
    y = _rms(op_ref[...] + ob_ref[...], nw_ref[...]) * (g * _sigmoid(g))
    y_ref[...] = y.astype(y_ref.dtype)


def _rec_bwd(qb, kb, vb, tb, proj_h, o_part, norm_w, d_hgrn):
    s = proj_h.shape[1]
    nh = d_hgrn // HEAD_DIM
    rows = BWD_BLOCKS * TOK_BLOCK
    nstep = s // rows
    blk = (rows, HEAD_DIM)
    hblk = (None,) + blk
    rev = lambda h, i: (h, nstep - 1 - i, 0)

    def col(group):
        return pl.BlockSpec(hblk, lambda h, i, g=_group_index(group): (g * nh + h, nstep - 1 - i, 0))

    return pl.pallas_call(
        _rec_bwd_kernel,
        grid=(nh, nstep),
        in_specs=[
            pl.BlockSpec(hblk, rev),
            pl.BlockSpec(hblk, rev),
            pl.BlockSpec(hblk, rev),
            pl.BlockSpec((None, BWD_BLOCKS * CHUNKS_PER_BLOCK, HEAD_DIM), rev),
            pl.BlockSpec(hblk, rev),
            col(GROUP_GATE),
            pl.BlockSpec((1, HEAD_DIM), lambda h, i: (0, h)),
        ],
        out_specs=pl.BlockSpec(blk, lambda h, i: (nstep - 1 - i, h)),
        out_shape=jax.ShapeDtypeStruct((s, d_hgrn), BF16),
        scratch_shapes=[pltpu.VMEM(blk, F32), pltpu.VMEM((HEAD_DIM, HEAD_DIM), F32)],
        compiler_params=_params(("parallel", "arbitrary")),
        name="rec_bwd",
    )(qb, kb, vb, tb, o_part, proj_h, norm_w)


def _conv_kernel(b_ref, c_ref, h_ref, cp_ref, hp_ref, cn_ref, hn_ref, w_ref, y_ref):
    i = pl.program_id(0)
    last = pl.num_programs(0) - 1
    n = c_ref.shape[1]
    sub = lax.broadcasted_iota(jnp.int32, (SUBLANES, LANES), 0)
    for c in range(c_ref.shape[0]):
        u = c_ref[c].astype(F32) * h_ref[c].astype(F32)
        halo_p = (cp_ref[c].astype(F32) * hp_ref[c].astype(F32))[HALO_ROWS - SUBLANES:]
        halo_p = jnp.where(i > 0, halo_p, 0.0)
        first = jnp.where(sub == 0, pltpu.roll(halo_p, 1, 0), pltpu.roll(u[n - SUBLANES:], 1, 0))
        u_prev = jnp.concatenate([first, u[:n - SUBLANES]], axis=0)
        halo_n = (cn_ref[c].astype(F32) * hn_ref[c].astype(F32))[:SUBLANES]
        halo_n = jnp.where(i < last, halo_n, 0.0)
        lastv = jnp.where(sub == SUBLANES - 1, pltpu.roll(halo_n, SUBLANES - 1, 0),
                          pltpu.roll(u[:SUBLANES], SUBLANES - 1, 0))
        u_next = jnp.concatenate([u[SUBLANES:], lastv], axis=0)
        w = w_ref[:, c * LANES:(c + 1) * LANES]
        y = w[0:1] * u_prev + w[1:2] * u + w[2:3] * u_next
        y_ref[:, c * LANES:(c + 1) * LANES] = (b_ref[c].astype(F32) * y).astype(y_ref.dtype)


def _conv(proj_h, conv_w, d_hgrn, d_conv, *, tc):
    s = proj_h.shape[1]
    nblk = s // TOK_BLOCK
    nhalo = s // HALO_ROWS
    per_blk = TOK_BLOCK // HALO_ROWS
    slabs = tc // LANES
    ncol = d_conv // tc

    def first_block(group):
        return _group_index(group) * (d_hgrn // tc)

    def main(group):
        return pl.BlockSpec((slabs, TOK_BLOCK, LANES),
                            lambda i, j, b=first_block(group): (b + j, i, 0))

    def prev(group):
        return pl.BlockSpec((slabs, HALO_ROWS, LANES),
                            lambda i, j, b=first_block(group): (b + j,
                                                                jnp.maximum(i * per_blk - 1, 0), 0))

    def nxt(group):
        return pl.BlockSpec((slabs, HALO_ROWS, LANES),
                            lambda i, j, b=first_block(group): (
                                b + j, jnp.minimum((i + 1) * per_blk, nhalo - 1), 0))

    return pl.pallas_call(
        _conv_kernel,
        grid=(nblk, ncol),
        in_specs=[main(GROUP_CB), main(GROUP_CC), main(GROUP_CH), prev(GROUP_CC), prev(GROUP_CH),
                  nxt(GROUP_CC), nxt(GROUP_CH),
                  pl.BlockSpec((CONV_WIDTH, tc), lambda i, j: (0, j))],
        out_specs=pl.BlockSpec((TOK_BLOCK, tc), lambda i, j: (i, j)),
        out_shape=jax.ShapeDtypeStruct((s, d_conv), BF16),
        compiler_params=_params(("parallel", "parallel")),
        name="short_conv",
    )(proj_h, proj_h, proj_h, proj_h, proj_h, proj_h, proj_h, conv_w)


def _out_proj_kernel(x_ref, yr_ref, yc_ref, wr_ref, wc_ref, o_ref):
    acc = jnp.dot(yr_ref[...], wr_ref[...], preferred_element_type=F32)
    acc = acc + jnp.dot(yc_ref[...], wc_ref[...], preferred_element_type=F32)
    o_ref[...] = x_ref[...] + acc


def _out_proj(x, y_rec, y_conv, w, layer, *, tm):
    s, d = x.shape
    d_rec = y_rec.shape[1]
    d_conv = y_conv.shape[1]
    assert d_rec == d_conv
    return pl.pallas_call(
        _out_proj_kernel,
        grid=(s // tm,),
        in_specs=[
            pl.BlockSpec((tm, d), lambda i: (i, 0)),
            pl.BlockSpec((tm, d_rec), lambda i: (i, 0)),
            pl.BlockSpec((tm, d_conv), lambda i: (i, 0)),
            pl.BlockSpec((None, d_rec, d), lambda i: (layer, 0, 0)),
            pl.BlockSpec((None, d_conv, d), lambda i: (layer, 1, 0)),
        ],
        out_specs=pl.BlockSpec((tm, d), lambda i: (i, 0)),
        out_shape=jax.ShapeDtypeStruct((s, d), F32),
        compiler_params=_params(("parallel",)),
        name="out_proj",
    )(x, y_rec, y_conv, w, w)


def _ffn_kernel(final, x_ref, nw_ref, wg_ref, wu_ref, wd_ref, fw_ref, o_ref, h_scr):
    f = pl.program_id(1)

    def tile(h):
        g = jnp.dot(h, wg_ref[...], preferred_element_type=F32)
        u = jnp.dot(h, wu_ref[...], preferred_element_type=F32)
        a = (g * _sigmoid(g) * u).astype(BF16)
        return jnp.dot(a, wd_ref[...], preferred_element_type=F32)

    @pl.when(f == 0)
    def _():
        rows = x_ref.shape[0] // NORM_SPLIT
        for m in range(NORM_SPLIT):
            sl = pl.ds(m * rows, rows)
            x = x_ref[sl, :]
            h = _rms(x, nw_ref[...]).astype(BF16)
            h_scr[sl, :] = h
            o_ref[sl, :] = x + tile(h)

    @pl.when(f > 0)
    def _():
        o_ref[...] += tile(h_scr[...])

    if final:
        @pl.when(f == pl.num_programs(1) - 1)
        def _():
            o_ref[...] = _rms(o_ref[...], fw_ref[...])


def _ffn(x, nw, w_gu, w_d, fw, layer, final, *, tm, tf):
    s, d = x.shape
    d_ff = w_d.shape[1]
    nf = d_ff // tf
    return pl.pallas_call(
        functools.partial(_ffn_kernel, final),
        grid=(s // tm, nf),
        in_specs=[
            pl.BlockSpec((tm, d), lambda i, f: (i, 0)),
            pl.BlockSpec((1, d), lambda i, f: (0, 0)),
            pl.BlockSpec((None, d, tf), lambda i, f: (layer, 0, f)),
            pl.BlockSpec((None, d, tf), lambda i, f: (layer, 0, nf + f)),
            pl.BlockSpec((None, tf, d), lambda i, f: (layer, f, 0)),
            pl.BlockSpec((1, d), lambda i, f: (0, 0)),
        ],
        out_specs=pl.BlockSpec((tm, d), lambda i, f: (i, 0)),
        out_shape=jax.ShapeDtypeStruct((s, d), F32),
        scratch_shapes=[pltpu.VMEM((tm, d), BF16)],
        compiler_params=_params(("parallel", "arbitrary")),
        name="ffn",
    )(x, nw, w_gu, w_gu, w_d, fw)


def _interleave(x):
    s, d = x.shape
    x4 = x.reshape(s // TOK_BLOCK, CHUNKS_PER_BLOCK, CHUNK, d)
    return x4.transpose(0, 2, 1, 3).reshape(s, d)


def _deinterleave(x):
    s, d = x.shape
    x4 = x.reshape(s // TOK_BLOCK, CHUNK, CHUNKS_PER_BLOCK, d)
    return x4.transpose(0, 2, 1, 3).reshape(s, d)


def kernel(x, attn_norm_w, w_in, lb_fwd, lb_bwd, hgrn_norm_w, conv_w, w_out, ffn_norm_w,
           w_gate_up, w_down, final_norm_w):
    bsz, s, d = x.shape
    depth = w_in.shape[0]
    d_hgrn = lb_fwd.shape[1]
    d_conv = conv_w.shape[2]
    assert bsz == 1 and s % (BWD_BLOCKS * TOK_BLOCK) == 0
    assert w_in.shape[2] == 5 * d_hgrn + 3 * d_conv and d_hgrn == d_conv

    xs = _interleave(x[0])
    w_in, w_out, w_gate_up, w_down = (_cast_bf16(w) for w in (w_in, w_out, w_gate_up, w_down))
    fw = final_norm_w.reshape(1, d)
    for l in range(depth):
        proj_f, proj_h = _in_proj(xs, attn_norm_w[l].reshape(1, d), w_in, l, tm=1024, tn=d_hgrn)
        o_part, qb, kb, vb, tb = _rec_fwd(proj_f, proj_h, lb_fwd, lb_bwd, l, d_hgrn)
        y_rec = _rec_bwd(qb, kb, vb, tb, proj_h, o_part, hgrn_norm_w[l].reshape(1, d_hgrn), d_hgrn)
        y_conv = _conv(proj_h, conv_w[l], d_hgrn, d_conv, tc=d_conv)
        xs = _out_proj(xs, y_rec, y_conv, w_out, l, tm=512)
        xs = _ffn(xs, ffn_norm_w[l].reshape(1, d), w_gate_up, w_down, fw, l, l == depth - 1,
                  tm=1024, tf=512)
    return _deinterleave(xs)[None]
```

```python
import functools

import jax
import jax.numpy as jnp
from jax import lax
from jax.experimental import pallas as pl
from jax.experimental.pallas import tpu as pltpu

F32 = jnp.float32
BF16 = jnp.bfloat16

EPS = 1e-6
HEAD_DIM = 128
CHUNK = 128
SUBLANES = 8
LANES = 128
CHUNKS_PER_BLOCK = SUBLANES
TOK_BLOCK = CHUNK * CHUNKS_PER_BLOCK
POS_BLOCK = SUBLANES
N_GROUPS = CHUNK // POS_BLOCK
GROUP_ROWS = POS_BLOCK * CHUNKS_PER_BLOCK
LOW_LEVELS = tuple(1 << b for b in range(POS_BLOCK.bit_length() - 1))
HIGH_SPANS = tuple(1 << b for b in range(N_GROUPS.bit_length() - 1))
N_LOW = len(LOW_LEVELS) + 1
N_LEVELS = N_LOW + len(HIGH_SPANS)
N_FAC_LEVEL = 2 * (len(HIGH_SPANS) - 1)
FAC_QF, FAC_KF, FAC_QB, FAC_KB = (N_FAC_LEVEL + i for i in range(4))
FAC_RQF, FAC_RKF, FAC_RQB, FAC_RKB = (N_FAC_LEVEL + 4 + i for i in range(4))
N_FAC = N_FAC_LEVEL + 8
FAC_ROWS = N_FAC * N_GROUPS * CHUNKS_PER_BLOCK
BASE_QP, BASE_KS, BASE_QS, BASE_KP, BASE_QI, BASE_KI = range(6)
N_BASE = 6
RATIO_SPAN_FIRST = 4
RATIO_SPAN_LATER = N_GROUPS // 2
RATIO_MIN = 2.0 ** -100
BWD_BLOCKS = 8
GROUP_Q, GROUP_ZF, GROUP_ZB, GROUP_V, GROUP_GATE, GROUP_CB, GROUP_CC, GROUP_CH = range(8)
F32_GROUP0, N_F32_GROUPS = GROUP_ZF, 3
NORM_SPLIT = 4
CONV_WIDTH = 3
HALO_ROWS = 2 * SUBLANES
VMEM_LIMIT = 58 * 1024 * 1024
CAST_BLOCK_BYTES = 12 * 1024 * 1024

_NT = (((1,), (1,)), ((), ()))
_TN = (((0,), (0,)), ((), ()))


def _sigmoid(x):
    return 0.5 * jnp.tanh(0.5 * x) + 0.5


def _rms(x, w):
    ms = jnp.mean(x * x, axis=-1, keepdims=True)
    return x * lax.rsqrt(ms + EPS) * w


def _params(sem, fusible=None):
    return pltpu.CompilerParams(dimension_semantics=sem, vmem_limit_bytes=VMEM_LIMIT,
                                allow_input_fusion=fusible)


def _cast_kernel(w_ref, o_ref):
    o_ref[...] = w_ref[...].astype(o_ref.dtype)


def _cast_bf16(w):
    depth, r, c = w.shape
    rows = depth * r
    tr = 1 << ((CAST_BLOCK_BYTES // (4 * c)).bit_length() - 1)
    assert tr >= 2 * SUBLANES and rows % tr == 0
    out = pl.pallas_call(
        _cast_kernel,
        grid=(rows // tr,),
        in_specs=[pl.BlockSpec((tr, c), lambda i: (i, 0))],
        out_specs=pl.BlockSpec((tr, c), lambda i: (i, 0)),
        out_shape=jax.ShapeDtypeStruct((rows, c), BF16),
        compiler_params=_params(("parallel",)),
        name="cast_bf16",
    )(w.reshape(rows, c))
    return out.reshape(depth, r, c)


def _in_proj_kernel(x_ref, nw_ref, w_ref, of_ref, oh_ref, h_scr):
    j = pl.program_id(1)

    def project(o_ref, h, sl):
        res = jnp.dot(h, w_ref[...], preferred_element_type=F32)
        for c in range(o_ref.shape[0]):
            o_ref[c, sl, :] = res[:, c * LANES:(c + 1) * LANES].astype(o_ref.dtype)

    @pl.when(j == 0)
    def _():
        rows = x_ref.shape[0] // NORM_SPLIT
        for m in range(NORM_SPLIT):
            sl = pl.ds(m * rows, rows)
            h = _rms(x_ref[sl, :], nw_ref[...]).astype(BF16)
            h_scr[sl, :] = h
            project(oh_ref, h, sl)

    wide = jnp.logical_and(j >= F32_GROUP0, j < F32_GROUP0 + N_F32_GROUPS)
    narrow = jnp.logical_and(j > 0, jnp.logical_not(wide))
    pl.when(wide)(lambda: project(of_ref, h_scr[...], slice(None)))
    pl.when(narrow)(lambda: project(oh_ref, h_scr[...], slice(None)))


def _in_proj(x, nw, w, layer, *, tm, tn):
    s, d = x.shape
    n = w.shape[2]
    ngroups = n // tn
    slabs = tn // LANES
    assert F32_GROUP0 == 1 and ngroups > F32_GROUP0 + N_F32_GROUPS
    nhalf = ngroups - N_F32_GROUPS
    return pl.pallas_call(
        _in_proj_kernel,
        grid=(s // tm, ngroups),
        in_specs=[
            pl.BlockSpec((tm, d), lambda i, j: (i, 0)),
            pl.BlockSpec((1, d), lambda i, j: (0, 0)),
            pl.BlockSpec((None, d, tn), lambda i, j: (layer, 0, j)),
        ],
        out_specs=[
            pl.BlockSpec((slabs, tm, LANES),
                         lambda i, j: (jnp.clip(j - F32_GROUP0, 0, N_F32_GROUPS - 1), i, 0)),
            pl.BlockSpec((slabs, tm, LANES),
                         lambda i, j: (jnp.clip(j - N_F32_GROUPS, 0, nhalf - 1), i, 0)),
        ],
        out_shape=[jax.ShapeDtypeStruct((N_F32_GROUPS * slabs, s, LANES), F32),
                   jax.ShapeDtypeStruct((nhalf * slabs, s, LANES), BF16)],
        scratch_shapes=[pltpu.VMEM((tm, d), BF16)],
        compiler_params=_params(("parallel", "arbitrary")),
        name="in_proj",
    )(x, nw, w)


def _lower_bound(lb_ref, layer):
    p = lb_ref[...]
    if layer == 0:
        return jnp.zeros((1, p.shape[1]), F32)
    e = jnp.exp(p - jnp.max(p, axis=0, keepdims=True))
    sm = e / jnp.sum(e, axis=0, keepdims=True)
    return jnp.sum(sm[1:layer + 1], axis=0, keepdims=True)


def _gates(z, lb):
    th = 0.5 * jnp.tanh(0.5 * z)
    half = 0.5 * (1.0 - lb)
    f = (lb + half) + (1.0 - lb) * th
    k = half - (1.0 - lb) * th
    return f, k


def _halves(x, h):
    x5 = x.reshape((x.shape[0] // (2 * h), 2, h) + x.shape[1:])
    return x5[:, 0], x5[:, 1]


def _join(a, b):
    x = jnp.stack([a, b], axis=1)
    return x.reshape((x.shape[0] * x.shape[1] * x.shape[2],) + a.shape[2:])


def _chunk_rows(ref, r):
    return ref[pl.ds(r, CHUNK, stride=CHUNKS_PER_BLOCK), :]


def _fac_rows(k, g):
    return pl.ds((k * N_GROUPS + g) * CHUNKS_PER_BLOCK, CHUNKS_PER_BLOCK)


def _group_factor(fac_ref, k, r):
    return jnp.concatenate(
        [jnp.broadcast_to(fac_ref[pl.ds((k * N_GROUPS + g) * CHUNKS_PER_BLOCK + r, 1), :],
                          (POS_BLOCK, HEAD_DIM))
         for g in range(N_GROUPS)], axis=0)


def _by_group_parity(even, odd, span):
    pieces = []
    for g in range(N_GROUPS):
        src = odd if (g // span) % 2 else even
        pieces.append(src[g * POS_BLOCK:(g + 1) * POS_BLOCK])
    return jnp.concatenate(pieces, axis=0)


def _seg_products(tot, span):
    ones = jnp.ones_like(tot[0])
    pre, suf = [None] * N_GROUPS, [None] * N_GROUPS
    for s0 in range(0, N_GROUPS, span):
        acc = None
        for g in range(s0, s0 + span):
            pre[g] = ones if acc is None else acc
            acc = tot[g] if acc is None else acc * tot[g]
        acc = None
        for g in reversed(range(s0, s0 + span)):
            suf[g] = ones if acc is None else acc
            acc = tot[g] if acc is None else acc * tot[g]
    return pre, suf


def _group_scan(g, q_ref, zf_ref, zb_ref, lbf, lbb, low_stage=None):
    sh = (POS_BLOCK, CHUNKS_PER_BLOCK, HEAD_DIM)
    flat = (GROUP_ROWS, HEAD_DIM)
    rows = pl.ds(g * GROUP_ROWS, GROUP_ROWS)
    q = q_ref[rows, :].astype(F32).reshape(sh)
    qs = q * _sigmoid(q)
    f_f, k_f = _gates(zf_ref[rows, :].reshape(sh), lbf)
    f_b, k_b = _gates(zb_ref[rows, :].reshape(sh), lbb)
    if low_stage is not None:
        stq_ref, stk_ref = low_stage
        stq_ref[0, rows, :] = qs.reshape(flat)
        stk_ref[0, rows, :] = (k_f + k_b).reshape(flat)
    pin, sex, tf = f_f, jnp.ones(sh, F32), f_f
    sin, pex, tb = f_b, jnp.ones(sh, F32), f_b
    for li, h in enumerate(LOW_LEVELS):
        pin1, pin2 = _halves(pin, h)
        sex1, sex2 = _halves(sex, h)
        sin1, sin2 = _halves(sin, h)
        pex1, pex2 = _halves(pex, h)
        if low_stage is not None:
            qs1, qs2 = _halves(qs, h)
            kf1, _ = _halves(k_f, h)
            _, kb2 = _halves(k_b, h)
            stq_ref[li + 1, rows, :] = _join(qs1 * sin1, qs2 * pin2).reshape(flat)
            stk_ref[li + 1, rows, :] = _join(kf1 * sex1, kb2 * pex2).reshape(flat)
        tf1, tf2 = _halves(tf, 1)
        tb1, tb2 = _halves(tb, 1)
        pin = _join(pin1, pin2 * tf1)
        sex = _join(sex1 * tf2, sex2)
        sin = _join(sin1 * tb2, sin2)
        pex = _join(pex1, pex2 * tb1)
        tf = (tf1 * tf2).reshape((tf.shape[0] // 2,) + sh[1:])
        tb = (tb1 * tb2).reshape((tb.shape[0] // 2,) + sh[1:])
    return rows, qs, k_f, k_b, pin, sex, sin, pex, tf[0], tb[0]


def _chunk_factors(tot_f, tot_b, fac_ref, tf_ref, tb_ref):
    pre_f, suf_f = _seg_products(tot_f, N_GROUPS)
    pre_b, suf_b = _seg_products(tot_b, N_GROUPS)
    for g in range(N_GROUPS):
        fac_ref[_fac_rows(FAC_QF, g), :] = pre_f[g]
        fac_ref[_fac_rows(FAC_KF, g), :] = suf_f[g]
        fac_ref[_fac_rows(FAC_QB, g), :] = suf_b[g]
        fac_ref[_fac_rows(FAC_KB, g), :] = pre_b[g]
    tf_ref[...] = pre_f[N_GROUPS - 1] * tot_f[N_GROUPS - 1]
    tb_ref[...] = pre_b[N_GROUPS - 1] * tot_b[N_GROUPS - 1]


def _level_factors(tot_f, tot_b, fac_ref, si):
    span = HIGH_SPANS[si]
    pre_f, suf_f = _seg_products(tot_f, span)
    pre_b, suf_b = _seg_products(tot_b, span)
    for g in range(N_GROUPS):
        second = (g // span) % 2
        fac_ref[_fac_rows(2 * (si - 1), g), :] = pre_f[g] if second else suf_b[g]
        fac_ref[_fac_rows(2 * (si - 1) + 1, g), :] = pre_b[g] if second else suf_f[g]


def _scan_ratio(layer, lbf_ref, lbb_ref, q_ref, zf_ref, zb_ref, tb_ref, stage, span):
    _, _, base_ref, fac_ref, tf_ref = stage
    lbf = _lower_bound(lbf_ref, layer)
    lbb = _lower_bound(lbb_ref, layer)
    flat = (GROUP_ROWS, HEAD_DIM)
    tot_f, tot_b = [], []
    for g in range(N_GROUPS):
        rows, qs, k_f, k_b, pin, sex, sin, pex, tf, tb = _group_scan(
            g, q_ref, zf_ref, zb_ref, lbf, lbb)
        base_ref[BASE_QP, rows, :] = (qs * pin).reshape(flat)
        base_ref[BASE_KS, rows, :] = (k_f * sex).reshape(flat)
        base_ref[BASE_QS, rows, :] = (qs * sin).reshape(flat)
        base_ref[BASE_KP, rows, :] = (k_b * pex).reshape(flat)
        if (g // span) % 2:
            base_ref[BASE_QI, rows, :] = (qs * (1.0 / pex)).reshape(flat)
            base_ref[BASE_KI, rows, :] = (k_f * (1.0 / pin)).reshape(flat)
        else:
            base_ref[BASE_QI, rows, :] = (qs * (1.0 / sex)).reshape(flat)
            base_ref[BASE_KI, rows, :] = (k_b * (1.0 / sin)).reshape(flat)
        tot_f.append(tf)
        tot_b.append(tb)

    pre_f, suf_f = _seg_products(tot_f, span)
    pre_b, suf_b = _seg_products(tot_b, span)
    smallest = None
    for g in range(N_GROUPS):
        second = (g // span) % 2
        fac_ref[_fac_rows(FAC_RQF, g), :] = pre_f[g] if second else 1.0 / suf_f[g]
        fac_ref[_fac_rows(FAC_RKF, g), :] = 1.0 / pre_f[g] if second else suf_f[g]
        fac_ref[_fac_rows(FAC_RQB, g), :] = 1.0 / pre_b[g] if second else suf_b[g]
        fac_ref[_fac_rows(FAC_RKB, g), :] = pre_b[g] if second else 1.0 / suf_b[g]
        if g % span == span - 1:
            run = jnp.minimum(pre_f[g] * tot_f[g], pre_b[g] * tot_b[g])
            smallest = run if smallest is None else jnp.minimum(smallest, run)
    if 2 * span < N_GROUPS:
        _level_factors(tot_f, tot_b, fac_ref, len(HIGH_SPANS) - 1)
    _chunk_factors(tot_f, tot_b, fac_ref, tf_ref, tb_ref)
    return jnp.min(smallest)


def _scan_levels(layer, lbf_ref, lbb_ref, q_ref, zf_ref, zb_ref, tb_ref, stage):
    stq_ref, stk_ref, base_ref, fac_ref, tf_ref = stage
    lbf = _lower_bound(lbf_ref, layer)
    lbb = _lower_bound(lbb_ref, layer)
    flat = (GROUP_ROWS, HEAD_DIM)
    tot_f, tot_b = [], []
    for g in range(N_GROUPS):
        rows, qs, k_f, k_b, pin, sex, sin, pex, tf, tb = _group_scan(
            g, q_ref, zf_ref, zb_ref, lbf, lbb, (stq_ref, stk_ref))
        base_ref[BASE_QP, rows, :] = (qs * pin).reshape(flat)
        base_ref[BASE_KS, rows, :] = (k_f * sex).reshape(flat)
        base_ref[BASE_QS, rows, :] = (qs * sin).reshape(flat)
        base_ref[BASE_KP, rows, :] = (k_b * pex).reshape(flat)
        tot_f.append(tf)
        tot_b.append(tb)
    for si in range(1, len(HIGH_SPANS)):
        _level_factors(tot_f, tot_b, fac_ref, si)
    _chunk_factors(tot_f, tot_b, fac_ref, tf_ref, tb_ref)


def _chunk_bases(base_ref, r):
    return tuple(_chunk_rows(base_ref.at[k], r) for k in (BASE_QP, BASE_KS, BASE_QS, BASE_KP))


def _high_level(bases, fac_ref, si, r):
    qp, ks, qs, kp = bases
    span = HIGH_SPANS[si]
    ql = _by_group_parity(qs, qp, span)
    kl = _by_group_parity(ks, kp, span)
    if si > 0:
        ql = ql * _group_factor(fac_ref, 2 * (si - 1), r)
        kl = kl * _group_factor(fac_ref, 2 * (si - 1) + 1, r)
    return ql.astype(BF16), kl.astype(BF16)


def _nt_dot(a, b):
    return lax.dot_general(a, b, _NT, preferred_element_type=F32)


def _finish_chunk(r, a, bases, v_ref, o_ref, bwd_refs, st_ref, fac_ref, tf_ref):
    qp, ks, qs, kp = bases
    vb = _chunk_rows(v_ref, r).astype(BF16)
    st = st_ref[...]
    o = jnp.dot(a.astype(BF16), vb, preferred_element_type=F32)
    qf = (qp * _group_factor(fac_ref, FAC_QF, r)).astype(BF16)
    o = o + _nt_dot(qf, st.astype(BF16))
    kf = (ks * _group_factor(fac_ref, FAC_KF, r)).astype(BF16)
    upd = lax.dot_general(vb, kf, _TN, preferred_element_type=F32)
    st_ref[...] = st * tf_ref[pl.ds(r, 1), :] + upd
    o_ref[pl.ds(r, CHUNK, stride=CHUNKS_PER_BLOCK), :] = o
    rows = pl.ds(r * CHUNK, CHUNK)
    qb_ref, kb_ref, vb_ref = bwd_refs
    qb_ref[rows, :] = (qs * _group_factor(fac_ref, FAC_QB, r)).astype(BF16)
    kb_ref[rows, :] = (kp * _group_factor(fac_ref, FAC_KB, r)).astype(BF16)
    vb_ref[rows, :] = vb


def _matmul_ratio(v_ref, o_ref, bwd_refs, st_ref, stage, span):
    _, _, base_ref, fac_ref, tf_ref = stage
    row = lax.broadcasted_iota(jnp.int32, (CHUNK, CHUNK), 0)
    col = lax.broadcasted_iota(jnp.int32, (CHUNK, CHUNK), 1)
    whole_chunk = 2 * span == N_GROUPS
    far = (row ^ col) >= 2 * span * POS_BLOCK
    top = len(HIGH_SPANS) - 1
    for r in range(CHUNKS_PER_BLOCK):
        bases = _chunk_bases(base_ref, r)
        qp, ks, qs, kp = bases
        qi = _chunk_rows(base_ref.at[BASE_QI], r)
        ki = _chunk_rows(base_ref.at[BASE_KI], r)
        qf = _by_group_parity(qi, qp, span) * _group_factor(fac_ref, FAC_RQF, r)
        kf = _by_group_parity(ks, ki, span) * _group_factor(fac_ref, FAC_RKF, r)
        qb = _by_group_parity(qs, qi, span) * _group_factor(fac_ref, FAC_RQB, r)
        kb = _by_group_parity(ki, kp, span) * _group_factor(fac_ref, FAC_RKB, r)
        p_f = _nt_dot(qf.astype(BF16), kf.astype(BF16))
        p_b = _nt_dot(qb.astype(BF16), kb.astype(BF16))
        a = jnp.where(row >= col, p_f, p_b)
        if not whole_chunk:
            a = jnp.where(far, _nt_dot(*_high_level(bases, fac_ref, top, r)), a)
        a = a + jnp.where(row == col, p_b, 0.0)
        _finish_chunk(r, a, bases, v_ref, o_ref, bwd_refs, st_ref, fac_ref, tf_ref)


def _matmul_levels(v_ref, o_ref, bwd_refs, st_ref, stage):
    stq_ref, stk_ref, base_ref, fac_ref, tf_ref = stage
    row = lax.broadcasted_iota(jnp.int32, (CHUNK, CHUNK), 0)
    col = lax.broadcasted_iota(jnp.int32, (CHUNK, CHUNK), 1)
    xr = row ^ col
    lvl = jnp.zeros((CHUNK, CHUNK), jnp.int32)
    for b in range(N_LEVELS - 1):
        lvl = lvl + jnp.where(xr >= (1 << b), 1, 0)

    for r in range(CHUNKS_PER_BLOCK):
        bases = _chunk_bases(base_ref, r)
        a = None
        for li in range(N_LEVELS):
            if li < N_LOW:
                ql = _chunk_rows(stq_ref.at[li], r).astype(BF16)
                kl = _chunk_rows(stk_ref.at[li], r).astype(BF16)
            else:
                ql, kl = _high_level(bases, fac_ref, li - N_LOW, r)
            p = _nt_dot(ql, kl)
            a = p if a is None else jnp.where(lvl == li, p, a)
        _finish_chunk(r, a, bases, v_ref, o_ref, bwd_refs, st_ref, fac_ref, tf_ref)


def _rec_fwd_kernel(layer, lbf_ref, lbb_ref, q_ref, zf_ref, zb_ref, v_ref,
                    o_ref, qb_ref, kb_ref, vb_ref, tb_ref, *scratch):
    stage, st_ref = scratch[:-1], scratch[-1]
    bwd_refs = (qb_ref, kb_ref, vb_ref)
    span = RATIO_SPAN_FIRST if layer == 0 else RATIO_SPAN_LATER

    @pl.when(pl.program_id(1) == 0)
    def _():
        st_ref[...] = jnp.zeros_like(st_ref)

    smallest = _scan_ratio(layer, lbf_ref, lbb_ref, q_ref, zf_ref, zb_ref, tb_ref, stage, span)
    safe = smallest >= RATIO_MIN

    @pl.when(safe)
    def _():
        _matmul_ratio(v_ref, o_ref, bwd_refs, st_ref, stage, span)

    @pl.when(jnp.logical_not(safe))
    def _():
        _scan_levels(layer, lbf_ref, lbb_ref, q_ref, zf_ref, zb_ref, tb_ref, stage)
        _matmul_levels(v_ref, o_ref, bwd_refs, st_ref, stage)


def _group_index(group):
    if F32_GROUP0 <= group < F32_GROUP0 + N_F32_GROUPS:
        return group - F32_GROUP0
    return group if group < F32_GROUP0 else group - N_F32_GROUPS


def _rec_fwd(proj_f, proj_h, lb_fwd, lb_bwd, layer, d_hgrn):
    s = proj_f.shape[1]
    depth = lb_fwd.shape[0]
    nh = d_hgrn // HEAD_DIM
    nblk = s // TOK_BLOCK
    blk = (TOK_BLOCK, HEAD_DIM)
    hblk = (None,) + blk
    here = lambda h, i: (h, i, 0)

    def col(group):
        return pl.BlockSpec(hblk, lambda h, i, g=_group_index(group): (g * nh + h, i, 0))

    lb_spec = pl.BlockSpec((depth, HEAD_DIM), lambda h, i: (0, h))
    return pl.pallas_call(
        functools.partial(_rec_fwd_kernel, layer),
        grid=(nh, nblk),
        in_specs=[lb_spec, lb_spec, col(GROUP_Q), col(GROUP_ZF), col(GROUP_ZB), col(GROUP_V)],
        out_specs=[
            pl.BlockSpec(hblk, here),
            pl.BlockSpec(hblk, here),
            pl.BlockSpec(hblk, here),
            pl.BlockSpec(hblk, here),
            pl.BlockSpec((None, CHUNKS_PER_BLOCK, HEAD_DIM), here),
        ],
        out_shape=[jax.ShapeDtypeStruct((nh, s, HEAD_DIM), F32),
                   jax.ShapeDtypeStruct((nh, s, HEAD_DIM), BF16),
                   jax.ShapeDtypeStruct((nh, s, HEAD_DIM), BF16),
                   jax.ShapeDtypeStruct((nh, s, HEAD_DIM), BF16),
                   jax.ShapeDtypeStruct((nh, nblk * CHUNKS_PER_BLOCK, HEAD_DIM), F32)],
        scratch_shapes=[
            pltpu.VMEM((N_LOW,) + blk, F32),
            pltpu.VMEM((N_LOW,) + blk, F32),
            pltpu.VMEM((N_BASE,) + blk, F32),
            pltpu.VMEM((FAC_ROWS, HEAD_DIM), F32),
            pltpu.VMEM((CHUNKS_PER_BLOCK, HEAD_DIM), F32),
            pltpu.VMEM((HEAD_DIM, HEAD_DIM), F32),
        ],
        compiler_params=_params(("parallel", "arbitrary")),
        name="rec_fwd",
    )(lb_fwd, lb_bwd, proj_h, proj_f, proj_f, proj_f)


def _rec_bwd_kernel(qb_ref, kb_ref, vb_ref, tb_ref, op_ref, g_ref, nw_ref, y_ref, ob_ref, st_ref):
    @pl.when(pl.program_id(1) == 0)
    def _():
        st_ref[...] = jnp.zeros_like(st_ref)

    for blk in reversed(range(BWD_BLOCKS)):
        for r in reversed(range(CHUNKS_PER_BLOCK)):
            st = st_ref[...]
            rows = pl.ds(blk * TOK_BLOCK + r * CHUNK, CHUNK)
            strided = pl.ds(blk * TOK_BLOCK + r, CHUNK, stride=CHUNKS_PER_BLOCK)
            ob_ref[strided, :] = _nt_dot(qb_ref[rows, :], st.astype(BF16))
            upd = lax.dot_general(vb_ref[rows, :], kb_ref[rows, :], _TN,
                                  preferred_element_type=F32)
            st_ref[...] = st * tb_ref[pl.ds(blk * CHUNKS_PER_BLOCK + r, 1), :] + upd
    g = g_ref[...].astype(F32)
    y = _rms(op_ref[...] + ob_ref[...], nw_ref[...]) * (g * _sigmoid(g))
    y_ref[...] = y.astype(y_ref.dtype)


def _rec_bwd(qb, kb, vb, tb, proj_h, o_part, norm_w, d_hgrn):
    s = proj_h.shape[1]
    nh = d_hgrn // HEAD_DIM
    rows = BWD_BLOCKS * TOK_BLOCK
    nstep = s // rows
    blk = (rows, HEAD_DIM)
    hblk = (None,) + blk
    rev = lambda h, i: (h, nstep - 1 - i, 0)

    def col(group):
        return pl.BlockSpec(hblk, lambda h, i, g=_group_index(group): (g * nh + h, nstep - 1 - i, 0))

    return pl.pallas_call(
        _rec_bwd_kernel,
        grid=(nh, nstep),
        in_specs=[
            pl.BlockSpec(hblk, rev),
            pl.BlockSpec(hblk, rev),
            pl.BlockSpec(hblk, rev),
            pl.BlockSpec((None, BWD_BLOCKS * CHUNKS_PER_BLOCK, HEAD_DIM), rev),
            pl.BlockSpec(hblk, rev),
            col(GROUP_GATE),
            pl.BlockSpec((1, HEAD_DIM), lambda h, i: (0, h)),
        ],
        out_specs=pl.BlockSpec(blk, lambda h, i: (nstep - 1 - i, h)),
        out_shape=jax.ShapeDtypeStruct((s, d_hgrn), BF16),
        scratch_shapes=[pltpu.VMEM(blk, F32), pltpu.VMEM((HEAD_DIM, HEAD_DIM), F32)],
        compiler_params=_params(("parallel", "arbitrary")),
        name="rec_bwd",
    )(qb, kb, vb, tb, o_part, proj_h, norm_w)


def _conv_kernel(b_ref, c_ref, h_ref, cp_ref, hp_ref, cn_ref, hn_ref, w_ref, y_ref):
    i = pl.program_id(0)
    last = pl.num_programs(0) - 1
    n = c_ref.shape[1]
    sub = lax.broadcasted_iota(jnp.int32, (SUBLANES, LANES), 0)
    for c in range(c_ref.shape[0]):
        u = c_ref[c].astype(F32) * h_ref[c].astype(F32)
        halo_p = (cp_ref[c].astype(F32) * hp_ref[c].astype(F32))[HALO_ROWS - SUBLANES:]
        halo_p = jnp.where(i > 0, halo_p, 0.0)
        first = jnp.where(sub == 0, pltpu.roll(halo_p, 1, 0), pltpu.roll(u[n - SUBLANES:], 1, 0))
        u_prev = jnp.concatenate([first, u[:n - SUBLANES]], axis=0)
        halo_n = (cn_ref[c].astype(F32) * hn_ref[c].astype(F32))[:SUBLANES]
        halo_n = jnp.where(i < last, halo_n, 0.0)
        lastv = jnp.where(sub == SUBLANES - 1, pltpu.roll(halo_n, SUBLANES - 1, 0),
                          pltpu.roll(u[:SUBLANES], SUBLANES - 1, 0))
        u_next = jnp.concatenate([u[SUBLANES:], lastv], axis=0)
        w = w_ref[:, c * LANES:(c + 1) * LANES]
        y = w[0:1] * u_prev + w[1:2] * u + w[2:3] * u_next
        y_ref[:, c * LANES:(c + 1) * LANES] = (b_ref[c].astype(F32) * y).astype(y_ref.dtype)


def _conv(proj_h, conv_w, d_hgrn, d_conv, *, tc):
    s = proj_h.shape[1]
    nblk = s // TOK_BLOCK
    nhalo = s // HALO_ROWS
    per_blk = TOK_BLOCK // HALO_ROWS
    slabs = tc // LANES
    ncol = d_conv // tc

    def first_block(group):
        return _group_index(group) * (d_hgrn // tc)

    def main(group):
        return pl.BlockSpec((slabs, TOK_BLOCK, LANES),
                            lambda i, j, b=first_block(group): (b + j, i, 0))

    def prev(group):
        return pl.BlockSpec((slabs, HALO_ROWS, LANES),
                            lambda i, j, b=first_block(group): (b + j,
                                                                jnp.maximum(i * per_blk - 1, 0), 0))

    def nxt(group):
        return pl.BlockSpec((slabs, HALO_ROWS, LANES),
                            lambda i, j, b=first_block(group): (
                                b + j, jnp.minimum((i + 1) * per_blk, nhalo - 1), 0))

    return pl.pallas_call(
        _conv_kernel,
        grid=(nblk, ncol),
        in_specs=[main(GROUP_CB), main(GROUP_CC), main(GROUP_CH), prev(GROUP_CC), prev(GROUP_CH),
                  nxt(GROUP_CC), nxt(GROUP_CH),
                  pl.BlockSpec((CONV_WIDTH, tc), lambda i, j: (0, j))],
        out_specs=pl.BlockSpec((TOK_BLOCK, tc), lambda i, j: (i, j)),
        out_shape=jax.ShapeDtypeStruct((s, d_conv), BF16),
        compiler_params=_params(("parallel", "parallel")),
        name="short_conv",
    )(proj_h, proj_h, proj_h, proj_h, proj_h, proj_h, proj_h, conv_w)


def _out_proj_kernel(x_ref, yr_ref, yc_ref, wr_ref, wc_ref, o_ref):
    acc = jnp.dot(yr_ref[...], wr_ref[...], preferred_element_type=F32)
    acc = acc + jnp.dot(yc_ref[...], wc_ref[...], preferred_element_type=F32)
    o_ref[...] = x_ref[...] + acc


def _out_proj(x, y_rec, y_conv, w, layer, *, tm):
    s, d = x.shape
    d_rec = y_rec.shape[1]
    d_conv = y_conv.shape[1]
    assert d_rec == d_conv
    return pl.pallas_call(
        _out_proj_kernel,
        grid=(s // tm,),
        in_specs=[
            pl.BlockSpec((tm, d), lambda i: (i, 0)),
            pl.BlockSpec((tm, d_rec), lambda i: (i, 0)),
            pl.BlockSpec((tm, d_conv), lambda i: (i, 0)),
            pl.BlockSpec((None, d_rec, d), lambda i: (layer, 0, 0)),
            pl.BlockSpec((None, d_conv, d), lambda i: (layer, 1, 0)),
        ],
        out_specs=pl.BlockSpec((tm, d), lambda i: (i, 0)),
        out_shape=jax.ShapeDtypeStruct((s, d), F32),
        compiler_params=_params(("parallel",), [False, False, False, True, True]),
        name="out_proj",
    )(x, y_rec, y_conv, w, w)


def _ffn_kernel(final, x_ref, nw_ref, wg_ref, wu_ref, wd_ref, fw_ref, o_ref, h_scr):
    f = pl.program_id(1)

    def tile(h):
        g = jnp.dot(h, wg_ref[...], preferred_element_type=F32)
        u = jnp.dot(h, wu_ref[...], preferred_element_type=F32)
        a = (g * _sigmoid(g) * u).astype(BF16)
        return jnp.dot(a, wd_ref[...], preferred_element_type=F32)

    @pl.when(f == 0)
    def _():
        rows = x_ref.shape[0] // NORM_SPLIT
        for m in range(NORM_SPLIT):
            sl = pl.ds(m * rows, rows)
            x = x_ref[sl, :]
            h = _rms(x, nw_ref[...]).astype(BF16)
            h_scr[sl, :] = h
            o_ref[sl, :] = x + tile(h)

    @pl.when(f > 0)
    def _():
        o_ref[...] += tile(h_scr[...])

    if final:
        @pl.when(f == pl.num_programs(1) - 1)
        def _():
            o_ref[...] = _rms(o_ref[...], fw_ref[...])


def _ffn(x, nw, w_gu, w_d, fw, layer, final, *, tm, tf):
    s, d = x.shape
    d_ff = w_d.shape[1]
    nf = d_ff // tf
    return pl.pallas_call(
        functools.partial(_ffn_kernel, final),
        grid=(s // tm, nf),
        in_specs=[
            pl.BlockSpec((tm, d), lambda i, f: (i, 0)),
            pl.BlockSpec((1, d), lambda i, f: (0, 0)),
            pl.BlockSpec((None, d, tf), lambda i, f: (layer, 0, f)),
            pl.BlockSpec((None, d, tf), lambda i, f: (layer, 0, nf + f)),
            pl.BlockSpec((None, tf, d), lambda i, f: (layer, f, 0)),
            pl.BlockSpec((1, d), lambda i, f: (0, 0)),
        ],
        out_specs=pl.BlockSpec((tm, d), lambda i, f: (i, 0)),
        out_shape=jax.ShapeDtypeStruct((s, d), F32),
        scratch_shapes=[pltpu.VMEM((tm, d), BF16)],
        compiler_params=_params(("parallel", "arbitrary"), [False, False, True, True, True, False]),
        name="ffn",
    )(x, nw, w_gu, w_gu, w_d, fw)


def _interleave(x):
    s, d = x.shape
    x4 = x.reshape(s // TOK_BLOCK, CHUNKS_PER_BLOCK, CHUNK, d)
    return x4.transpose(0, 2, 1, 3).reshape(s, d)


def _deinterleave(x):
    s, d = x.shape
    x4 = x.reshape(s // TOK_BLOCK, CHUNK, CHUNKS_PER_BLOCK, d)
    return x4.transpose(0, 2, 1, 3).reshape(s, d)


def kernel(x, attn_norm_w, w_in, lb_fwd, lb_bwd, hgrn_norm_w, conv_w, w_out, ffn_norm_w,
           w_gate_up, w_down, final_norm_w):
    bsz, s, d = x.shape
    depth = w_in.shape[0]
    d_hgrn = lb_fwd.shape[1]
    d_conv = conv_w.shape[2]
    assert bsz == 1 and s % (BWD_BLOCKS * TOK_BLOCK) == 0
    assert w_in.shape[2] == 5 * d_hgrn + 3 * d_conv and d_hgrn == d_conv

    xs = _interleave(x[0])
    w_in = _cast_bf16(w_in)
    w_out, w_gate_up, w_down = (w.astype(BF16) for w in (w_out, w_gate_up, w_down))
    fw = final_norm_w.reshape(1, d)
    for l in range(depth):
        proj_f, proj_h = _in_proj(xs, attn_norm_w[l].reshape(1, d), w_in, l, tm=1024, tn=d_hgrn)
        o_part, qb, kb, vb, tb = _rec_fwd(proj_f, proj_h, lb_fwd, lb_bwd, l, d_hgrn)
        y_rec = _rec_bwd(qb, kb, vb, tb, proj_h, o_part, hgrn_norm_w[l].reshape(1, d_hgrn), d_hgrn)
        y_conv = _conv(proj_h, conv_w[l], d_hgrn, d_conv, tc=d_conv)
        xs = _out_proj(xs, y_rec, y_conv, w_out, l, tm=512)
        xs = _ffn(xs, ffn_norm_w[l].reshape(1, d), w_gate_up, w_down, fw, l, l == depth - 1,
                  tm=1024, tf=512)
    return _deinterleave(xs)[None]
```
